```python
import math
import jax, jax.numpy as jnp
from jax import lax
import numpy as np

D_MODEL = 1024
BATCH = 8
SEQ = 2048
DEPTH = 4

N_BRANCHES = 3
SSM_WIDTH = D_MODEL // 2
SSM_GROUP = 16
SSM_GROUPS = SSM_WIDTH // SSM_GROUP
SSM_STATE = 64
SSM_DT_MIN = 1e-3
SSM_DT_MAX = 1e-1
MLSTM_WIDTH = D_MODEL // 2
MLSTM_HEADS = 4
MLSTM_HEAD_DIM = MLSTM_WIDTH // MLSTM_HEADS
MLSTM_CHUNK = 128
CONV_WIDTH = 4
MOBA_WIDTH = D_MODEL // 2
MOBA_HEADS = 8
MOBA_HEAD_DIM = MOBA_WIDTH // MOBA_HEADS
MOBA_BLOCK = 256
MOBA_TOPK = 3
MOBA_QUERY_BLOCK = 32
REL_BUCKETS = 32
REL_MAX_DIST = 128
D_FF = 4 * D_MODEL
RMS_EPS = 1e-6
NEG_INF = -1e30

IN_SIZES = (SSM_WIDTH,
            MLSTM_WIDTH, MLSTM_WIDTH, MLSTM_WIDTH, MLSTM_WIDTH, MLSTM_HEADS, MLSTM_HEADS,
            MOBA_WIDTH, MOBA_WIDTH, MOBA_WIDTH,
            N_BRANCHES * D_MODEL)
IN_COLS = sum(IN_SIZES)

kernel_name = "hybrid_s5_mlstm_moba_gated_block"


def rms_norm(x, g):
    xf = x.astype(jnp.float32)
    y = xf * lax.rsqrt(jnp.mean(xf * xf, axis=-1, keepdims=True) + RMS_EPS)
    return (y * g.astype(jnp.float32)).astype(x.dtype)


def split_cols(t, sizes):
    outs, start = [], 0
    for s in sizes:
        outs.append(t[..., start:start + s])
        start += s
    return outs


def causal_conv(x, w):
    k, c = w.shape
    return lax.conv_general_dilated(x, w[:, None, :], window_strides=(1,), padding=[(k - 1, 0)],
                                    dimension_numbers=('NWC', 'WIO', 'NWC'), feature_group_count=c)


def _ssm_combine(left, right):
    a1r, a1i, b1r, b1i = left
    a2r, a2i, b2r, b2i = right
    return (a2r * a1r - a2i * a1i, a2r * a1i + a2i * a1r,
            a2r * b1r - a2i * b1i + b2r, a2r * b1i + a2i * b1r + b2i)


def ssm_branch(u, a_re, a_im, log_dt, b_re, b_im, c_re, c_im, d_skip, w_glu):
    f32 = jnp.float32
    bsz, s, w = u.shape
    uf = u.astype(f32)
    ug = uf.reshape(bsz, s, SSM_GROUPS, SSM_GROUP)
    ar, ai = a_re.astype(f32), a_im.astype(f32)
    dt = jnp.exp(log_dt.astype(f32))[:, None]
    decay = jnp.exp(dt * ar)
    abar_r, abar_i = decay * jnp.cos(dt * ai), decay * jnp.sin(dt * ai)
    den = ar * ar + ai * ai
    nr, ni = abar_r - 1.0, abar_i
    fr, fi = (nr * ar + ni * ai) / den, (ni * ar - nr * ai) / den
    br, bi = b_re.astype(f32), b_im.astype(f32)
    bbar_r = fr[..., None] * br - fi[..., None] * bi
    bbar_i = fr[..., None] * bi + fi[..., None] * br
    bu_r = jnp.einsum('bsgh,gph->bsgp', ug, bbar_r)
    bu_i = jnp.einsum('bsgh,gph->bsgp', ug, bbar_i)
    a_seq_r = jnp.broadcast_to(abar_r, (1, s, SSM_GROUPS, SSM_STATE))
    a_seq_i = jnp.broadcast_to(abar_i, (1, s, SSM_GROUPS, SSM_STATE))
    _, _, st_r, st_i = lax.associative_scan(_ssm_combine, (a_seq_r, a_seq_i, bu_r, bu_i), axis=1)
    y = (jnp.einsum('bsgp,ghp->bsgh', st_r, c_re.astype(f32))
         - jnp.einsum('bsgp,ghp->bsgh', st_i, c_im.astype(f32)))
    y = y.reshape(bsz, s, w) + d_skip.astype(f32) * uf
    yg = jax.nn.gelu(y)
    out = yg * jax.nn.sigmoid(yg @ w_glu.astype(f32))
    return out.astype(u.dtype)


def mlstm_branch(q, k, v, o_pre, i_pre, f_pre, i_bias, f_bias, head_gain):
    f32 = jnp.float32
    bsz, s, w = q.shape
    nh, dh, lc = MLSTM_HEADS, MLSTM_HEAD_DIM, MLSTM_CHUNK
    nc = s // lc

    def heads(t):
        return t.astype(f32).reshape(bsz, nc, lc, nh, dh).transpose(0, 3, 1, 2, 4)

    def gates(t, bias):
        return (t.astype(f32) + bias.astype(f32)).reshape(bsz, nc, lc, nh).transpose(0, 3, 1, 2)

    qh, kh, vh = heads(q), heads(k) * (dh ** -0.5), heads(v)
    ig = gates(i_pre, i_bias)
    lf = jax.nn.log_sigmoid(gates(f_pre, f_bias))
    bcum = jnp.cumsum(lf, axis=-1)
    gtot = bcum[..., -1]

    causal = jnp.tril(jnp.ones((lc, lc), dtype=bool))
    log_d = jnp.where(causal, bcum[..., :, None] - bcum[..., None, :] + ig[..., None, :], -jnp.inf)
    m_intra = jnp.max(log_d, axis=-1)

    log_w = gtot[..., None] - bcum + ig
    m_loc = jnp.max(log_w, axis=-1)
    wgt = jnp.exp(log_w - m_loc[..., None])
    d_c = jnp.einsum('bhcsv,bhcsk->bhcvk', wgt[..., None] * vh, kh)
    d_n = jnp.einsum('bhcs,bhcsk->bhck', wgt, kh)

    def step(carry, xs):
        c_st, n_st, m_st = carry
        dc, dn, g, ml = xs
        m_new = jnp.maximum(g + m_st, ml)
        a = jnp.exp(g + m_st - m_new)
        bb = jnp.exp(ml - m_new)
        c_new = a[..., None, None] * c_st + bb[..., None, None] * dc
        n_new = a[..., None] * n_st + bb[..., None] * dn
        return (c_new, n_new, m_new), (c_st, n_st, m_st)

    init = (jnp.zeros((bsz, nh, dh, dh), f32), jnp.zeros((bsz, nh, dh), f32), jnp.zeros((bsz, nh), f32))
    xs = (jnp.moveaxis(d_c, 2, 0), jnp.moveaxis(d_n, 2, 0), jnp.moveaxis(gtot, 2, 0), jnp.moveaxis(m_loc, 2, 0))
    _, (c_prev, n_prev, m_prev) = lax.scan(step, init, xs)
    c_prev = jnp.moveaxis(c_prev, 0, 2)
    n_prev = jnp.moveaxis(n_prev, 0, 2)
    m_prev = jnp.moveaxis(m_prev, 0, 2)

    log_inter = bcum + m_prev[..., None]
    m_q = jnp.maximum(log_inter, m_intra)
    s_mat = jnp.einsum('bhcjd,bhcsd->bhcjs', qh, kh) * jnp.exp(log_d - m_q[..., None])
    inter = jnp.exp(log_inter - m_q)
    num = (jnp.einsum('bhcjs,bhcsv->bhcjv', s_mat, vh)
           + inter[..., None] * jnp.einsum('bhcvk,bhcjk->bhcjv', c_prev, qh))
    den = jnp.sum(s_mat, axis=-1) + inter * jnp.einsum('bhck,bhcjk->bhcj', n_prev, qh)
    h = num / jnp.maximum(jnp.abs(den), jnp.exp(-m_q))[..., None]
    h = h * lax.rsqrt(jnp.mean(h * h, axis=-1, keepdims=True) + RMS_EPS)
    h = h.transpose(0, 2, 3, 1, 4).reshape(bsz, s, w) * head_gain.astype(f32)
    out = jax.nn.sigmoid(o_pre.astype(f32)) * h
    return out.astype(q.dtype)


def t5_bucket(dist):
    max_exact = REL_BUCKETS // 2
    is_small = dist < max_exact
    large = max_exact + (jnp.log(jnp.maximum(dist, 1).astype(jnp.float32) / max_exact)
                         / math.log(REL_MAX_DIST / max_exact) * (REL_BUCKETS - max_exact)).astype(jnp.int32)
    large = jnp.minimum(large, REL_BUCKETS - 1)
    return jnp.where(is_small, dist, large)


def moba_branch(q, k, v, rel_bias):
    f32 = jnp.float32
    bsz, s, w = q.shape
    nh, dh, bs, qb = MOBA_HEADS, MOBA_HEAD_DIM, MOBA_BLOCK, MOBA_QUERY_BLOCK
    nb = -(-s // bs)
    s_pad = nb * bs
    topk = min(MOBA_TOPK, nb)
    nq = s // qb
    scale = dh ** -0.5

    def heads(t):
        return t.reshape(bsz, s, nh, dh).transpose(0, 2, 1, 3)

    pad = ((0, 0), (0, 0), (0, s_pad - s), (0, 0))
    kb = jnp.pad(heads(k), pad).reshape(bsz, nh, nb, bs, dh)
    vb = jnp.pad(heads(v), pad).reshape(bsz, nh, nb, bs, dh)
    k_mean = jnp.mean(kb.astype(f32), axis=3)
    q_blocks = heads(q).reshape(bsz, nh, nq, qb, dh).transpose(2, 0, 1, 3, 4)
    bias_tab = rel_bias.astype(f32)
    b_idx = jnp.arange(bsz)[:, None, None, None]
    h_idx = jnp.arange(nh)[None, :, None, None]
    blk_pos = jnp.arange(bs)

    def attend(args):
        qi, q_blk = args
        q_pos = qi * qb + jnp.arange(qb)
        cur = (qi * qb) // bs
        gate = jnp.einsum('bhqd,bhnd->bhqn', q_blk.astype(f32), k_mean)
        gate = jnp.where(jnp.arange(nb) < cur, gate, NEG_INF)
        _, sel = lax.top_k(gate, topk)
        sel_valid = (jnp.arange(topk) < cur)[:, None]
        k_sel = kb[b_idx, h_idx, sel]
        v_sel = vb[b_idx, h_idx, sel]
        lg_sel = jnp.einsum('bhqd,bhqjkd->bhqjk', q_blk, k_sel).astype(f32) * scale
        dist_sel = q_pos[None, None, :, None, None] - (sel[..., None] * bs + blk_pos)
        bias_sel = bias_tab[h_idx[..., None], t5_bucket(jnp.maximum(dist_sel, 0))]
        lg_sel = jnp.where(sel_valid, lg_sel + bias_sel, NEG_INF)
        k_own = lax.dynamic_slice_in_dim(kb, cur, 1, axis=2)[:, :, 0]
        v_own = lax.dynamic_slice_in_dim(vb, cur, 1, axis=2)[:, :, 0]
        lg_own = jnp.einsum('bhqd,bhkd->bhqk', q_blk, k_own).astype(f32) * scale
        dist_own = q_pos[:, None] - (cur * bs + blk_pos)[None, :]
        bias_own = bias_tab[:, t5_bucket(jnp.maximum(dist_own, 0))]
        lg_own = jnp.where(dist_own >= 0, lg_own + bias_own[None], NEG_INF)
        logits = jnp.concatenate([lg_sel.reshape(bsz, nh, qb, topk * bs), lg_own], axis=-1)
        p = jax.nn.softmax(logits, axis=-1).astype(v.dtype)
        p_sel = p[..., :topk * bs].reshape(bsz, nh, qb, topk, bs)
        p_own = p[..., topk * bs:]
        return (jnp.einsum('bhqjk,bhqjkd->bhqd', p_sel, v_sel)
                + jnp.einsum('bhqk,bhkd->bhqd', p_own, v_own))

    outs = lax.map(attend, (jnp.arange(nq), q_blocks))
    return outs.transpose(1, 0, 3, 2, 4).reshape(bsz, s, w).astype(q.dtype)


def setup_inputs(seed: int = 0) -> dict:
    key = jax.random.key(seed)
    ks = jax.random.split(key, 32)
    f32 = jnp.float32
    nrm = lambda k, shape, sc: jax.random.normal(k, shape, f32) * sc
    gain = lambda k: 1.0 + 0.02 * jax.random.normal(k, (DEPTH, D_MODEL), f32)
    n_idx = jnp.arange(SSM_STATE, dtype=f32)
    return {
        "x": jax.random.normal(ks[0], (BATCH, SEQ, D_MODEL), f32),
        "w_in": nrm(ks[1], (DEPTH, D_MODEL, IN_COLS), D_MODEL ** -0.5),
        "conv_w": nrm(ks[2], (DEPTH, CONV_WIDTH, 2 * MLSTM_WIDTH), CONV_WIDTH ** -0.5),
        "ssm_a_re": -0.5 + 0.01 * jax.random.normal(ks[3], (DEPTH, SSM_GROUPS, SSM_STATE), f32),
        "ssm_a_im": math.pi * n_idx + 0.01 * jax.random.normal(ks[4], (DEPTH, SSM_GROUPS, SSM_STATE), f32),
        "ssm_log_dt": jax.random.uniform(ks[5], (DEPTH, SSM_GROUPS), f32, math.log(SSM_DT_MIN), math.log(SSM_DT_MAX)),
        "ssm_b_re": nrm(ks[6], (DEPTH, SSM_GROUPS, SSM_STATE, SSM_GROUP), (2 * SSM_GROUP) ** -0.5),
        "ssm_b_im": nrm(ks[7], (DEPTH, SSM_GROUPS, SSM_STATE, SSM_GROUP), (2 * SSM_GROUP) ** -0.5),
        "ssm_c_re": nrm(ks[8], (DEPTH, SSM_GROUPS, SSM_GROUP, SSM_STATE), (2 * SSM_STATE) ** -0.5),
        "ssm_c_im": nrm(ks[9], (DEPTH, SSM_GROUPS, SSM_GROUP, SSM_STATE), (2 * SSM_STATE) ** -0.5),
        "ssm_d": nrm(ks[10], (DEPTH, SSM_WIDTH), 1.0),
        "ssm_w_glu": nrm(ks[11], (DEPTH, SSM_WIDTH, SSM_WIDTH), SSM_WIDTH ** -0.5),
        "mlstm_i_bias": nrm(ks[12], (DEPTH, MLSTM_HEADS), 0.1),
        "mlstm_f_bias": jnp.linspace(3.0, 6.0, MLSTM_HEADS, dtype=f32)[None, :] + nrm(ks[13], (DEPTH, MLSTM_HEADS), 0.1),
        "mlstm_head_gain": 1.0 + 0.02 * jax.random.normal(ks[14], (DEPTH, MLSTM_WIDTH), f32),
        "rel_bias": nrm(ks[15], (MOBA_HEADS, REL_BUCKETS), 0.5),
        "w_ssm_proj": nrm(ks[16], (DEPTH, SSM_WIDTH, D_MODEL), SSM_WIDTH ** -0.5),
        "w_mlstm_proj": nrm(ks[17], (DEPTH, MLSTM_WIDTH, D_MODEL), MLSTM_WIDTH ** -0.5),
        "w_moba_proj": nrm(ks[18], (DEPTH, MOBA_WIDTH, D_MODEL), MOBA_WIDTH ** -0.5),
        "w_out": nrm(ks[19], (DEPTH, D_MODEL, D_MODEL), D_MODEL ** -0.5),
        "w_ff1": nrm(ks[20], (DEPTH, D_MODEL, D_FF), D_MODEL ** -0.5),
        "w_ff2": nrm(ks[21], (DEPTH, D_FF, D_MODEL), D_FF ** -0.5),
        "norm_mix_pre": gain(ks[22]),
        "norm_mix_post": gain(ks[23]),
        "norm_ffn_pre": gain(ks[24]),
        "norm_ffn_post": gain(ks[25]),
    }


def reference(x, w_in, conv_w, ssm_a_re, ssm_a_im, ssm_log_dt, ssm_b_re, ssm_b_im, ssm_c_re, ssm_c_im,
              ssm_d, ssm_w_glu, mlstm_i_bias, mlstm_f_bias, mlstm_head_gain, rel_bias,
              w_ssm_proj, w_mlstm_proj, w_moba_proj, w_out, w_ff1, w_ff2,
              norm_mix_pre, norm_mix_post, norm_ffn_pre, norm_ffn_post):
    bsz, s, _ = x.shape
    for l in range(DEPTH):
        h = rms_norm(x, norm_mix_pre[l])
        proj = jnp.einsum('bsd,dc->bsc', h, w_in[l])
        (u_ssm, m_q, m_k, m_v, m_o, m_i, m_f, a_q, a_k, a_v, gate_pre) = split_cols(proj, IN_SIZES)
        qk = jax.nn.silu(causal_conv(jnp.concatenate([m_q, m_k], axis=-1), conv_w[l]))
        m_q, m_k = qk[..., :MLSTM_WIDTH], qk[..., MLSTM_WIDTH:]
        y_ssm = ssm_branch(u_ssm, ssm_a_re[l], ssm_a_im[l], ssm_log_dt[l], ssm_b_re[l], ssm_b_im[l],
                           ssm_c_re[l], ssm_c_im[l], ssm_d[l], ssm_w_glu[l])
        y_mlstm = mlstm_branch(m_q, m_k, m_v, m_o, m_i, m_f, mlstm_i_bias[l], mlstm_f_bias[l], mlstm_head_gain[l])
        y_moba = moba_branch(a_q, a_k, a_v, rel_bias)
        g = jax.nn.sigmoid(gate_pre.astype(jnp.float32)).astype(x.dtype).reshape(bsz, s, N_BRANCHES, D_MODEL)
        merged = (g[:, :, 0] * (y_ssm @ w_ssm_proj[l])
                  + g[:, :, 1] * (y_mlstm @ w_mlstm_proj[l])
                  + g[:, :, 2] * (y_moba @ w_moba_proj[l]))
        x = x + rms_norm(merged @ w_out[l], norm_mix_post[l])
        h = rms_norm(x, norm_ffn_pre[l])
        f = jnp.square(jax.nn.relu(h @ w_ff1[l])) @ w_ff2[l]
        x = x + rms_norm(f, norm_ffn_post[l])
    return x
```

```python
import functools
import math

import jax
import jax.numpy as jnp
from jax import lax
from jax.experimental import pallas as pl
from jax.experimental.pallas import tpu as pltpu

F32 = jnp.float32
BF16 = jnp.bfloat16

D_MODEL = 1024
SSM_WIDTH = 512
SSM_GROUP = 16
SSM_GROUPS = 32
SSM_STATE = 64
MLSTM_WIDTH = 512
MLSTM_HEADS = 4
MLSTM_HEAD_DIM = 128
MLSTM_CHUNK = 128
CONV_WIDTH = 4
MOBA_WIDTH = 512
MOBA_HEADS = 8
MOBA_HEAD_DIM = 64
MOBA_BLOCK = 256
MOBA_TOPK = 3
REL_BUCKETS = 32
REL_MAX_DIST = 128
D_FF = 4 * D_MODEL
RMS_EPS = 1e-6
NEG_INF = -1e30

LANES = 128
SUBLANES = 8
VMEM_LIMIT = 56 * 1024 * 1024

C_U = 0
C_QK = 512
C_MV = 1536
C_MO = 2048
C_AQ = 2560
C_AK = 3072
C_AV = 3584
C_GATE = 4096
C_GIF = 7168
C_END = 7296


def _rms(x, g):
    return x * lax.rsqrt(jnp.mean(x * x, axis=-1, keepdims=True) + RMS_EPS) * g


def _sigmoid(x):
    return 1.0 / (1.0 + jnp.exp(-x))


def _dot(a, b):
    return jnp.dot(a, b, preferred_element_type=F32)


def _dot_nt(a, b):
    return lax.dot_general(a, b, (((1,), (1,)), ((), ())), preferred_element_type=F32)


def _split3(a):
    hi = a.astype(BF16)
    r1 = a - hi.astype(F32)
    mid = r1.astype(BF16)
    lo = (r1 - mid.astype(F32)).astype(BF16)
    return hi, mid, lo


def _inproj_kernel(x_ref, g_ref, w_ref, u_ref, qk_ref, mv_ref, mo_ref, aq_ref, ak_ref, av_ref,
                   gate_ref, gif_ref):
    hb = _rms(x_ref[...], g_ref[...]).astype(BF16)

    def proj(a, b):
        return _dot(hb, w_ref[:, a:b])

    u_ref[...] = proj(C_U, C_QK)
    qk_ref[...] = proj(C_QK, C_MV)
    mv_ref[...] = proj(C_MV, C_MO)
    mo_ref[...] = proj(C_MO, C_AQ)
    aq_ref[...] = proj(C_AQ, C_AK)
    ak_ref[...] = proj(C_AK, C_AV)
    av_ref[...] = proj(C_AV, C_GATE)
    gate_ref[...] = _sigmoid(proj(C_GATE, C_GIF))
    gif_ref[...] = proj(C_GIF, C_END)


def _inproj(x2d, g, w, bsz, seq, ts):
    t = bsz * seq
    ns = seq // ts
    row = lambda b, s: (b * ns + s, 0)
    full = lambda b, s: (0, 0)

    def out(n):
        return jax.ShapeDtypeStruct((t, n), F32), pl.BlockSpec((ts, n), row)

    outs = [out(1024), out(512), out(512), out(512), out(512), out(512), out(3 * D_MODEL), out(LANES)]
    u_shape = jax.ShapeDtypeStruct((seq, bsz * SSM_WIDTH), F32)
    u_spec = pl.BlockSpec((ts, SSM_WIDTH), lambda b, s: (s, b))
    return pl.pallas_call(
        _inproj_kernel,
        grid=(bsz, ns),
        in_specs=[pl.BlockSpec((ts, D_MODEL), row),
                  pl.BlockSpec((1, D_MODEL), full),
                  pl.BlockSpec((D_MODEL, C_END), full, pipeline_mode=pl.Buffered(1))],
        out_specs=[u_spec] + [o[1] for o in outs],
        out_shape=[u_shape] + [o[0] for o in outs],
        compiler_params=pltpu.CompilerParams(dimension_semantics=("parallel", "parallel"),
                                             vmem_limit_bytes=VMEM_LIMIT),
        name="inproj",
    )(x2d, g, w)


SSM_KB = 2
SSM_KB_CH = SSM_WIDTH // SSM_KB
SSM_KB_ST = SSM_KB_CH // SSM_GROUP * SSM_STATE
SSM_SLAB = 512


def _gelu(x):
    return 0.5 * x * (1.0 + jnp.tanh(math.sqrt(2.0 / math.pi) * (x + 0.044715 * (x * x * x))))


def _ssm_kernel(u_ref, bblk_ref, cblk_ref, ar_ref, ai_ref, d_ref, wglu_ref, o_ref, xs_ref, st_ref, *, bsz, tc):
    @pl.when(pl.program_id(0) == 0)
    def _():
        st_ref[...] = jnp.zeros_like(st_ref)

    u = u_ref[...]
    ub = u.astype(BF16)
    for kb in range(SSM_KB):
        xs_ref[:, kb * 2 * SSM_KB_ST:(kb + 1) * 2 * SSM_KB_ST] = _dot(
            ub[:, kb * SSM_KB_CH:(kb + 1) * SSM_KB_CH], bblk_ref[kb])

    for kb in range(SSM_KB):
        for j in range(SSM_KB_ST // SSM_SLAB):
            re0 = kb * 2 * SSM_KB_ST + j * SSM_SLAB
            im0 = re0 + SSM_KB_ST
            co = kb * SSM_KB_ST + j * SSM_SLAB
            ar = ar_ref[:, co:co + SSM_SLAB]
            ai = ai_ref[:, co:co + SSM_SLAB]

            def body(t, carry, re0=re0, im0=im0, ar=ar, ai=ai):
                xr, xi = carry
                r0 = pl.multiple_of(t * bsz, bsz)
                br = xs_ref[pl.ds(r0, bsz), re0:re0 + SSM_SLAB]
                bi = xs_ref[pl.ds(r0, bsz), im0:im0 + SSM_SLAB]
                nr = ar * xr - ai * xi + br
                ni = ar * xi + ai * xr + bi
                xs_ref[pl.ds(r0, bsz), re0:re0 + SSM_SLAB] = nr
                xs_ref[pl.ds(r0, bsz), im0:im0 + SSM_SLAB] = ni
                return nr, ni

            xr, xi = lax.fori_loop(0, tc, body,
                                   (st_ref[:, re0:re0 + SSM_SLAB], st_ref[:, im0:im0 + SSM_SLAB]),
                                   unroll=8)
            st_ref[:, re0:re0 + SSM_SLAB] = xr
            st_ref[:, im0:im0 + SSM_SLAB] = xi

    ys = []
    for kb in range(SSM_KB):
        xb = xs_ref[:, kb * 2 * SSM_KB_ST:(kb + 1) * 2 * SSM_KB_ST].astype(BF16)
        ys.append(_dot(xb, cblk_ref[kb]))
    y = jnp.concatenate(ys, axis=-1) + d_ref[...] * u
    yg = _gelu(y)
    o_ref[...] = yg * _sigmoid(_dot(yg.astype(BF16), wglu_ref[...]))


def _ssm(u2, bblk, cblk, ar, ai, d, wglu, bsz, seq, tc):
    rows = tc * bsz
    full2 = lambda c: (0, 0)
    full3 = lambda c: (0, 0, 0)
    return pl.pallas_call(
        functools.partial(_ssm_kernel, bsz=bsz, tc=tc),
        grid=(seq // tc,),
        in_specs=[pl.BlockSpec((rows, SSM_WIDTH), lambda c: (c, 0)),
                  pl.BlockSpec(bblk.shape, full3),
                  pl.BlockSpec(cblk.shape, full3),
                  pl.BlockSpec(ar.shape, full2),
                  pl.BlockSpec(ai.shape, full2),
                  pl.BlockSpec((1, SSM_WIDTH), full2),
                  pl.BlockSpec((SSM_WIDTH, SSM_WIDTH), full2)],
        out_specs=pl.BlockSpec((rows, SSM_WIDTH), lambda c: (c, 0)),
        out_shape=jax.ShapeDtypeStruct((seq * bsz, SSM_WIDTH), F32),
        scratch_shapes=[pltpu.VMEM((rows, SSM_KB * 2 * SSM_KB_ST), F32),
                        pltpu.VMEM((bsz, SSM_KB * 2 * SSM_KB_ST), F32)],
        compiler_params=pltpu.CompilerParams(dimension_semantics=("arbitrary",),
                                             vmem_limit_bytes=VMEM_LIMIT),
        name="ssm",
    )(u2, bblk, cblk, ar, ai, d, wglu)


def _ssm_params(a_re, a_im, log_dt, b_re, b_im, c_re, c_im, bsz):
    ar, ai = a_re.astype(F32), a_im.astype(F32)
    dt = jnp.exp(log_dt.astype(F32))[:, None]
    decay = jnp.exp(dt * ar)
    abar_r, abar_i = decay * jnp.cos(dt * ai), decay * jnp.sin(dt * ai)
    den = ar * ar + ai * ai
    nr, ni = abar_r - 1.0, abar_i
    fr, fi = (nr * ar + ni * ai) / den, (ni * ar - nr * ai) / den
    br, bi = b_re.astype(F32), b_im.astype(F32)
    bbar_r = fr[..., None] * br - fi[..., None] * bi
    bbar_i = fr[..., None] * bi + fi[..., None] * br
    gl = SSM_GROUPS // SSM_KB
    eye = jnp.eye(gl, dtype=F32)

    def pack_b(bb):
        bb = bb.reshape(SSM_KB, gl, SSM_STATE, SSM_GROUP)
        dense = jnp.einsum('kgph,gq->kghqp', bb, eye)
        return dense.reshape(SSM_KB, gl * SSM_GROUP, gl * SSM_STATE)

    def pack_c(cc):
        cc = cc.reshape(SSM_KB, gl, SSM_GROUP, SSM_STATE)
        dense = jnp.einsum('kghp,gq->kgpqh', cc, eye)
        return dense.reshape(SSM_KB, gl * SSM_STATE, gl * SSM_GROUP)

    bblk = jnp.concatenate([pack_b(bbar_r), pack_b(bbar_i)], axis=-1).astype(BF16)
    cblk = jnp.concatenate([pack_c(c_re.astype(F32)), pack_c(-c_im.astype(F32))], axis=1).astype(BF16)
    ar_b = jnp.broadcast_to(abar_r.reshape(1, -1), (bsz, SSM_GROUPS * SSM_STATE))
    ai_b = jnp.broadcast_to(abar_i.reshape(1, -1), (bsz, SSM_GROUPS * SSM_STATE))
    return bblk, cblk, ar_b, ai_b


def _mlstm_kernel(qk_ref, v_ref, o_ref, gif_ref, convw_ref, bias_ref, gain_ref, out_ref,
                  xp_ref, ct_ref, n_ref, m_ref):
    lc, dh, nh = MLSTM_CHUNK, MLSTM_HEAD_DIM, MLSTM_HEADS
    halo = SUBLANES

    @pl.when(pl.program_id(1) == 0)
    def _():
        xp_ref[0:halo, :] = jnp.zeros((halo, 2 * MLSTM_WIDTH), F32)
        ct_ref[...] = jnp.zeros_like(ct_ref)
        n_ref[...] = jnp.zeros_like(n_ref)
        m_ref[...] = jnp.zeros_like(m_ref)

    xp_ref[halo:halo + lc, :] = qk_ref[...]
    conv = convw_ref[CONV_WIDTH - 1:CONV_WIDTH, :] * xp_ref[halo:halo + lc, :]
    for j in range(CONV_WIDTH - 1):
        off = halo - (CONV_WIDTH - 1) + j
        conv = conv + convw_ref[j:j + 1, :] * xp_ref[off:off + lc, :]
    xp_ref[0:halo, :] = xp_ref[lc:lc + halo, :]
    qk = conv * _sigmoid(conv)

    gpre = gif_ref[...] + bias_ref[...]
    lsig = jnp.minimum(gpre, 0.0) - jnp.log(1.0 + jnp.exp(-jnp.abs(gpre)))
    row_i = lax.broadcasted_iota(jnp.int32, (lc, lc), 0)
    col_i = lax.broadcasted_iota(jnp.int32, (lc, lc), 1)
    causal = col_i <= row_i
    tri = jnp.where(causal, 1.0, 0.0).astype(BF16)
    tri_t = jnp.where(row_i <= col_i, 1.0, 0.0).astype(BF16)
    l_hi, l_mid, l_lo = _split3(lsig)
    bcum_c = _dot(tri, l_hi) + _dot(tri, l_mid) + _dot(tri, l_lo)
    lsig_t = lsig.T
    t_hi, t_mid, t_lo = _split3(lsig_t)
    bcum_r = _dot(t_hi, tri_t) + _dot(t_mid, tri_t) + _dot(t_lo, tri_t)
    gpre_t = gpre.T

    for h in range(nh):
        qf = qk[:, h * dh:(h + 1) * dh]
        q = qf.astype(BF16)
        k = (qk[:, MLSTM_WIDTH + h * dh:MLSTM_WIDTH + (h + 1) * dh] * (dh ** -0.5))
        kb = k.astype(BF16)
        v = v_ref[:, h * dh:(h + 1) * dh]
        vb = v.astype(BF16)
        bc = bcum_c[:, nh + h:nh + h + 1]
        brow = bcum_r[nh + h:nh + h + 1, :]
        ic = gpre[:, h:h + 1]
        irow = gpre_t[h:h + 1, :]
        gtot = bc[lc - 1:lc, :]
        m_prev = m_ref[h:h + 1, 0:1]
        n_prev = n_ref[h:h + 1, :]
        ct_prev = ct_ref[h]

        log_d = jnp.where(causal, bc - brow + irow, NEG_INF)
        m_intra = jnp.max(log_d, axis=-1, keepdims=True)
        log_inter = bc + m_prev
        m_q = jnp.maximum(log_inter, m_intra)
        s_mat = _dot_nt(q, kb) * jnp.exp(log_d - m_q)
        inter = jnp.exp(log_inter - m_q)
        num =_dot(s_mat.astype(BF16), vb) + inter * _dot(q, ct_prev.astype(BF16))
        den = jnp.sum(s_mat, axis=-1, keepdims=True) + inter * jnp.sum(qf * n_prev, axis=-1, keepdims=True)
        hh = num / jnp.maximum(jnp.abs(den), jnp.exp(-m_q))
        hh = hh * lax.rsqrt(jnp.mean(hh * hh, axis=-1, keepdims=True) + RMS_EPS)
        og = _sigmoid(o_ref[:, h * dh:(h + 1) * dh])
        out_ref[:, h * dh:(h + 1) * dh] = og * (hh * gain_ref[:, h * dh:(h + 1) * dh])

        m_loc = jnp.max(gtot - brow + irow, axis=-1, keepdims=True)
        wgt_c = jnp.exp(gtot - bc + ic - m_loc)
        d_ct = _dot(k.T.astype(BF16), (wgt_c * v).astype(BF16))
        d_n = jnp.sum(wgt_c * k, axis=0, keepdims=True)
        m_new = jnp.maximum(gtot + m_prev, m_loc)
        a = jnp.exp(gtot + m_prev - m_new)
        bb = jnp.exp(m_loc - m_new)
        ct_ref[h] = a * ct_prev + bb * d_ct
        n_ref[h:h + 1, :] = a * n_prev + bb * d_n
        m_ref[h:h + 1, :] = jnp.broadcast_to(m_new, (1, LANES))


def _mlstm(qk, mv, mo, gif, convw, bias, gain, bsz, seq):
    t = bsz * seq
    lc = MLSTM_CHUNK
    nc = seq // lc
    row = lambda b, c: (b * nc + c, 0)
    full = lambda b, c: (0, 0)
    return pl.pallas_call(
        _mlstm_kernel,
        grid=(bsz, nc),
        in_specs=[pl.BlockSpec((lc, 2 * MLSTM_WIDTH), row),
                  pl.BlockSpec((lc, MLSTM_WIDTH), row),
                  pl.BlockSpec((lc, MLSTM_WIDTH), row),
                  pl.BlockSpec((lc, LANES), row),
                  pl.BlockSpec((CONV_WIDTH, 2 * MLSTM_WIDTH), full),
                  pl.BlockSpec((1, LANES), full),
                  pl.BlockSpec((1, MLSTM_WIDTH), full)],
        out_specs=pl.BlockSpec((lc, MLSTM_WIDTH), row),
        out_shape=jax.ShapeDtypeStruct((t, MLSTM_WIDTH), F32),
        scratch_shapes=[pltpu.VMEM((lc + SUBLANES, 2 * MLSTM_WIDTH), F32),
                        pltpu.VMEM((MLSTM_HEADS, MLSTM_HEAD_DIM, MLSTM_HEAD_DIM), F32),
                        pltpu.VMEM((SUBLANES, MLSTM_HEAD_DIM), F32),
                        pltpu.VMEM((SUBLANES, LANES), F32)],
        compiler_params=pltpu.CompilerParams(dimension_semantics=("parallel", "arbitrary"),
                                             vmem_limit_bytes=VMEM_LIMIT),
        name="mlstm",
    )(qk, mv, mo, gif, convw, bias, gain)


def _rel_bucket(dist):
    max_exact = REL_BUCKETS // 2
    is_small = dist < max_exact
    large = max_exact + (jnp.log(jnp.maximum(dist, 1).astype(F32) / max_exact)
                         / math.log(REL_MAX_DIST / max_exact) * (REL_BUCKETS - max_exact)).astype(jnp.int32)
    large = jnp.minimum(large, REL_BUCKETS - 1)
    return jnp.where(is_small, dist, large)


def _moba_bias_tiles(rel_bias):
    bs = MOBA_BLOCK
    i = jnp.arange(bs)[:, None]
    j = jnp.arange(bs)[None, :]
    tab = rel_bias.astype(F32)
    d_own = i - j
    own = jnp.where(d_own >= 0, tab[:, _rel_bucket(jnp.maximum(d_own, 0))], NEG_INF)
    prev = tab[:, _rel_bucket(bs + d_own)]
    far = tab[:, _rel_bucket(jnp.full((1, LANES), 2 * bs, jnp.int32))][:, 0, :]
    return jnp.stack([own, prev], axis=1), far


def _moba_kernel(q_ref, k_ref, v_ref, tiles_ref, far_ref, out_ref, kh_ref, vb_ref, km_ref, *, nb):
    bs, nh, dh = MOBA_BLOCK, MOBA_HEADS, MOBA_HEAD_DIM
    qi = pl.program_id(1)
    lane = lax.broadcasted_iota(jnp.int32, (1, LANES), 1)

    @pl.when(qi == 0)
    def _():
        for p in range(nh // 2):
            kp = k_ref[:, p * LANES:(p + 1) * LANES]
            kh_ref[2 * p] = jnp.where(lane < dh, kp, 0.0).astype(BF16)
            kh_ref[2 * p + 1] = jnp.where(lane >= dh, kp, 0.0).astype(BF16)
        vb_ref[...] = v_ref[...].astype(BF16)
        means = [jnp.mean(k_ref[n * bs:(n + 1) * bs, :], axis=0, keepdims=True) for n in range(nb)]
        means += [jnp.zeros_like(means[0])] * (SUBLANES - nb)
        kmean = jnp.concatenate(means, axis=0)
        rows = lax.broadcasted_iota(jnp.int32, (LANES, MOBA_WIDTH), 0)
        cols = lax.broadcasted_iota(jnp.int32, (LANES, MOBA_WIDTH), 1)
        tiled = jnp.concatenate([kmean] * (LANES // SUBLANES), axis=0)
        km_ref[...] = jnp.where(rows // SUBLANES == cols // dh, tiled, 0.0)

    q = q_ref[...]
    q_hi, q_mid, q_lo = _split3(q)
    m_hi, m_mid, m_lo = _split3(km_ref[...])
    gate = (_dot_nt(q_hi, m_hi) + _dot_nt(q_hi, m_mid) + _dot_nt(q_mid, m_hi)
            + _dot_nt(q_hi, m_lo) + _dot_nt(q_lo, m_hi) + _dot_nt(q_mid, m_mid))
    blk = lax.broadcasted_iota(jnp.int32, (bs, LANES), 1) % SUBLANES
    gate = jnp.where(blk < qi, gate, NEG_INF)
    rank = jnp.zeros((bs, LANES), F32)
    for s in range(1, SUBLANES):
        fwd = pltpu.roll(gate, LANES - s, 1)
        back = pltpu.roll(gate, SUBLANES - s, 1)
        wrapped = blk + s >= SUBLANES
        other = jnp.where(wrapped, back, fwd)
        rank = rank + jnp.where(wrapped, jnp.where(other >= gate, 1.0, 0.0), jnp.where(other > gate, 1.0, 0.0))
    selb = jnp.where(blk < qi, jnp.where(rank < MOBA_TOPK, 0.0, NEG_INF), NEG_INF)

    qs = (q * (dh ** -0.5)).astype(BF16)
    lane_b = lax.broadcasted_iota(jnp.int32, (bs, LANES), 1)
    own0 = pl.multiple_of(qi * bs, bs)
    for h in range(nh):
        p = h // 2
        qp = qs[:, p * LANES:(p + 1) * LANES]
        s0 = _dot_nt(qp, kh_ref[h, pl.ds(own0, bs), :]) + tiles_ref[h, 0]
        m0 = jnp.max(s0, axis=-1, keepdims=True)
        p0 = jnp.exp(s0 - m0)
        l0 = jnp.sum(p0, axis=-1, keepdims=True)
        acc0 = _dot(p0.astype(BF16), vb_ref[pl.ds(own0, bs), p * LANES:(p + 1) * LANES])
        far_h = far_ref[h:h + 1, 0:1]

        def body(n, carry, h=h, p=p, qp=qp, far_h=far_h):
            m, l, acc = carry
            r0 = pl.multiple_of(n * bs, bs)
            sb = jnp.sum(jnp.where(lane_b == h * SUBLANES + n, selb, 0.0), axis=-1, keepdims=True)
            bias = jnp.where(n == qi - 1, tiles_ref[h, 1], far_h)
            s = _dot_nt(qp, kh_ref[h, pl.ds(r0, bs), :]) + bias + sb
            m_new = jnp.maximum(m, jnp.max(s, axis=-1, keepdims=True))
            alpha = jnp.exp(m - m_new)
            pr = jnp.exp(s - m_new)
            l = alpha * l + jnp.sum(pr, axis=-1, keepdims=True)
            acc = alpha * acc + _dot(pr.astype(BF16), vb_ref[pl.ds(r0, bs), p * LANES:(p + 1) * LANES])
            return m_new, l, acc

        m, l, acc = lax.fori_loop(0, qi, body, (m0, l0, acc0))
        res = acc / l
        if h % 2 == 0:
            res_even = res
        else:
            out_ref[:, p * LANES:(p + 1) * LANES] = jnp.where(lane < dh, res_even, res)


def _moba(aq, ak, av, tiles, far, bsz, seq):
    t = bsz * seq
    bs = MOBA_BLOCK
    nb = seq // bs
    return pl.pallas_call(
        functools.partial(_moba_kernel, nb=nb),
        grid=(bsz, nb),
        in_specs=[pl.BlockSpec((bs, MOBA_WIDTH), lambda b, i: (b * nb + i, 0)),
                  pl.BlockSpec((seq, MOBA_WIDTH), lambda b, i: (b, 0)),
                  pl.BlockSpec((seq, MOBA_WIDTH), lambda b, i: (b, 0)),
                  pl.BlockSpec(tiles.shape, lambda b, i: (0, 0, 0, 0)),
                  pl.BlockSpec(far.shape, lambda b, i: (0, 0))],
        out_specs=pl.BlockSpec((bs, MOBA_WIDTH), lambda b, i: (b * nb + i, 0)),
        out_shape=jax.ShapeDtypeStruct((t, MOBA_WIDTH), F32),
        scratch_shapes=[pltpu.VMEM((MOBA_HEADS, seq, LANES), BF16),
                        pltpu.VMEM((seq, MOBA_WIDTH), BF16),
                        pltpu.VMEM((LANES, MOBA_WIDTH), F32)],
        compiler_params=pltpu.CompilerParams(dimension_semantics=("parallel", "arbitrary"),
                                             vmem_limit_bytes=VMEM_LIMIT),
        name="moba",
    )(aq, ak, av, tiles, far)


def _merge_kernel(x_ref, ys_ref, ym_ref, ya_ref, gate_ref, wp_ref, wo_ref, g_ref, o_ref):
    merged = (gate_ref[:, 0:D_MODEL] * _dot(ys_ref[...].astype(BF16), wp_ref[0])
              + gate_ref[:, D_MODEL:2 * D_MODEL] * _dot(ym_ref[...].astype(BF16), wp_ref[1])
              + gate_ref[:, 2 * D_MODEL:3 * D_MODEL] * _dot(ya_ref[...].astype(BF16), wp_ref[2]))
    z = _dot(merged.astype(BF16), wo_ref[...])
    o_ref[...] = x_ref[...] + _rms(z, g_ref[...])


def _merge(x2d, ys_t, ym, ya, gate, wp, wo, g, bsz, seq, ts):
    t = bsz * seq
    ns = seq // ts
    row = lambda b, s: (b * ns + s, 0)
    full = lambda b, s: (0, 0)
    return pl.pallas_call(
        _merge_kernel,
        grid=(bsz, ns),
        in_specs=[pl.BlockSpec((ts, D_MODEL), row),
                  pl.BlockSpec((ts, SSM_WIDTH), lambda b, s: (s, b)),
                  pl.BlockSpec((ts, MLSTM_WIDTH), row),
                  pl.BlockSpec((ts, MOBA_WIDTH), row),
                  pl.BlockSpec((ts, 3 * D_MODEL), row),
                  pl.BlockSpec(wp.shape, lambda b, s: (0, 0, 0)),
                  pl.BlockSpec(wo.shape, full),
                  pl.BlockSpec((1, D_MODEL), full)],
        out_specs=pl.BlockSpec((ts, D_MODEL), row),
        out_shape=jax.ShapeDtypeStruct((t, D_MODEL), F32),
        compiler_params=pltpu.CompilerParams(dimension_semantics=("parallel", "parallel"),
                                             vmem_limit_bytes=VMEM_LIMIT),
        name="merge",
    )(x2d, ys_t, ym, ya, gate, wp, wo, g)


def _ffn_kernel(x_ref, g1_ref, w1_ref, w2_ref, g2_ref, o_ref):
    x = x_ref[...]
    hb = _rms(x, g1_ref[...]).astype(BF16)
    a = jnp.maximum(_dot(hb, w1_ref[...]), 0.0)
    f = _dot((a * a).astype(BF16), w2_ref[...])
    o_ref[...] = x + _rms(f, g2_ref[...])


def _ffn(x2d, g1, w1, w2, g2, ts):
    t = x2d.shape[0]
    full = lambda i: (0, 0)
    return pl.pallas_call(
        _ffn_kernel,
        grid=(t // ts,),
        in_specs=[pl.BlockSpec((ts, D_MODEL), lambda i: (i, 0)),
                  pl.BlockSpec((1, D_MODEL), full),
                  pl.BlockSpec((D_MODEL, D_FF), full, pipeline_mode=pl.Buffered(1)),
                  pl.BlockSpec((D_FF, D_MODEL), full, pipeline_mode=pl.Buffered(1)),
                  pl.BlockSpec((1, D_MODEL), full)],
        out_specs=pl.BlockSpec((ts, D_MODEL), lambda i: (i, 0)),
        out_shape=jax.ShapeDtypeStruct((t, D_MODEL), F32),
        compiler_params=pltpu.CompilerParams(dimension_semantics=("parallel",),
                                             vmem_limit_bytes=VMEM_LIMIT),
        name="ffn",
    )(x2d, g1, w1, w2, g2)


def _reorder_w_in(w_in):
    w = w_in.astype(BF16)
    n_if = 2 * MLSTM_HEADS
    c_if = SSM_WIDTH + 4 * MLSTM_WIDTH
    gif = jnp.pad(w[..., c_if:c_if + n_if], ((0, 0), (0, 0), (0, LANES - n_if)))
    return jnp.concatenate([w[..., :c_if], w[..., c_if + n_if:], gif], axis=-1)


def kernel(x, w_in, conv_w, ssm_a_re, ssm_a_im, ssm_log_dt, ssm_b_re, ssm_b_im, ssm_c_re, ssm_c_im, ssm_d, ssm_w_glu, mlstm_i_bias, mlstm_f_bias, mlstm_head_gain, rel_bias, w_ssm_proj, w_mlstm_proj, w_moba_proj, w_out, w_ff1, w_ff2, norm_mix_pre, norm_mix_post, norm_ffn_pre, norm_ffn_post):
    bsz, seq, _ = x.shape
    depth = w_in.shape[0]
    assert bsz == SUBLANES and seq % MOBA_BLOCK == 0 and seq // MOBA_BLOCK <= SUBLANES
    ts = 256
    tc = 64
    x2d = x.reshape(bsz * seq, D_MODEL)
    w_in_r = _reorder_w_in(w_in)
    wp = jnp.stack([w_ssm_proj, w_mlstm_proj, w_moba_proj], axis=1).astype(BF16)
    wo = w_out.astype(BF16)
    w1 = w_ff1.astype(BF16)
    w2 = w_ff2.astype(BF16)
    wglu = ssm_w_glu.astype(BF16)
    gate_bias = jnp.pad(jnp.concatenate([mlstm_i_bias, mlstm_f_bias], axis=-1).astype(F32),
                        ((0, 0), (0, LANES - 2 * MLSTM_HEADS)))
    tiles, far = _moba_bias_tiles(rel_bias)
    for l in range(depth):
        u_t, qk, mv, mo, aq, ak, av, gate, gif = _inproj(
            x2d, norm_mix_pre[l][None, :], w_in_r[l], bsz, seq, ts)
        bblk, cblk, ar_b, ai_b = _ssm_params(ssm_a_re[l], ssm_a_im[l], ssm_log_dt[l], ssm_b_re[l], ssm_b_im[l],
                                             ssm_c_re[l], ssm_c_im[l], bsz)
        y_ssm = _ssm(u_t.reshape(seq * bsz, SSM_WIDTH), bblk, cblk, ar_b, ai_b, ssm_d[l][None, :], wglu[l],
                     bsz, seq, tc)
        y_mlstm = _mlstm(qk, mv, mo, gif, conv_w[l].astype(F32), gate_bias[l][None, :],
                         mlstm_head_gain[l][None, :].astype(F32), bsz, seq)
        y_moba = _moba(aq, ak, av, tiles, far, bsz, seq)
        x2d = _merge(x2d, y_ssm.reshape(seq, bsz * SSM_WIDTH), y_mlstm, y_moba, gate, wp[l], wo[l],
                     norm_mix_post[l][None, :], bsz, seq, ts)
        x2d = _ffn(x2d, norm_ffn_pre[l][None, :], w1[l], w2[l], norm_ffn_post[l][None, :], ts)
    return x2d.reshape(bsz, seq, D_MODEL)
```

```python
import functools
import math

import jax
import jax.numpy as jnp
from jax import lax
from jax.experimental import pallas as pl
from jax.experimental.pallas import tpu as pltpu

F32 = jnp.float32
BF16 = jnp.bfloat16

D_MODEL = 1024
SSM_WIDTH = 512
SSM_GROUP = 16
SSM_GROUPS = 32
SSM_STATE = 64
MLSTM_WIDTH = 512
MLSTM_HEADS = 4
MLSTM_HEAD_DIM = 128
MLSTM_CHUNK = 128
CONV_WIDTH = 4
MOBA_WIDTH = 512
MOBA_HEADS = 8
MOBA_HEAD_DIM = 64
MOBA_BLOCK = 256
MOBA_TOPK = 3
REL_BUCKETS = 32
REL_MAX_DIST = 128
D_FF = 4 * D_MODEL
RMS_EPS = 1e-6
NEG_INF = -1e30

LANES = 128
SUBLANES = 8
VMEM_LIMIT = 56 * 1024 * 1024

C_U = 0
C_QK = 512
C_MV = 1536
C_MO = 2048
C_AQ = 2560
C_AK = 3072
C_AV = 3584
C_GATE = 4096
C_GIF = 7168
C_END = 7296


def _rms(x, g):
    return x * lax.rsqrt(jnp.mean(x * x, axis=-1, keepdims=True) + RMS_EPS) * g


def _sigmoid(x):
    return 1.0 / (1.0 + jnp.exp(-x))


def _dot(a, b):
    return jnp.dot(a, b, preferred_element_type=F32)


def _dot_nt(a, b):
    return lax.dot_general(a, b, (((1,), (1,)), ((), ())), preferred_element_type=F32)


def _split3(a):
    hi = a.astype(BF16)
    r1 = a - hi.astype(F32)
    mid = r1.astype(BF16)
    lo = (r1 - mid.astype(F32)).astype(BF16)
    return hi, mid, lo


def _inproj_kernel(x_ref, g_ref, w_ref, u_ref, qk_ref, mv_ref, mo_ref, aq_ref, ak_ref, av_ref,
                   gate_ref, gif_ref):
    hb = _rms(x_ref[...], g_ref[...]).astype(BF16)

    def proj(a, b):
        return _dot(hb, w_ref[:, a:b])

    u_ref[...] = proj(C_U, C_QK)
    qk_ref[...] = proj(C_QK, C_MV)
    mv_ref[...] = proj(C_MV, C_MO)
    mo_ref[...] = proj(C_MO, C_AQ)
    aq_ref[...] = proj(C_AQ, C_AK)
    ak_ref[...] = proj(C_AK, C_AV)
    av_ref[...] = proj(C_AV, C_GATE)
    gate_ref[...] = _sigmoid(proj(C_GATE, C_GIF))
    gif_ref[...] = proj(C_GIF, C_END)


def _inproj(x2d, g, w, bsz, seq, ts):
    t = bsz * seq
    ns = seq // ts
    row = lambda b, s: (b * ns + s, 0)
    full = lambda b, s: (0, 0)

    def out(n):
        return jax.ShapeDtypeStruct((t, n), F32), pl.BlockSpec((ts, n), row)

    outs = [out(1024), out(512), out(512), out(512), out(512), out(512), out(3 * D_MODEL), out(LANES)]
    u_shape = jax.ShapeDtypeStruct((seq, bsz * SSM_WIDTH), F32)
    u_spec = pl.BlockSpec((ts, SSM_WIDTH), lambda b, s: (s, b))
    return pl.pallas_call(
        _inproj_kernel,
        grid=(bsz, ns),
        in_specs=[pl.BlockSpec((ts, D_MODEL), row),
                  pl.BlockSpec((1, D_MODEL), full),
                  pl.BlockSpec((D_MODEL, C_END), full, pipeline_mode=pl.Buffered(1))],
        out_specs=[u_spec] + [o[1] for o in outs],
        out_shape=[u_shape] + [o[0] for o in outs],
        compiler_params=pltpu.CompilerParams(dimension_semantics=("parallel", "parallel"),
                                             vmem_limit_bytes=VMEM_LIMIT),
        name="inproj",
    )(x2d, g, w)


SSM_KB = 2
SSM_KB_CH = SSM_WIDTH // SSM_KB
SSM_KB_ST = SSM_KB_CH // SSM_GROUP * SSM_STATE
SSM_SLAB = 512


def _gelu(x):
    return 0.5 * x * (1.0 + jnp.tanh(math.sqrt(2.0 / math.pi) * (x + 0.044715 * (x * x * x))))


def _ssm_kernel(u_ref, bblk_ref, cblk_ref, ar_ref, ai_ref, d_ref, wglu_ref, o_ref, xs_ref, st_ref, *, bsz, tc):
    @pl.when(pl.program_id(0) == 0)
    def _():
        st_ref[...] = jnp.zeros_like(st_ref)

    u = u_ref[...]
    ub = u.astype(BF16)
    for kb in range(SSM_KB):
        xs_ref[:, kb * 2 * SSM_KB_ST:(kb + 1) * 2 * SSM_KB_ST] = _dot(
            ub[:, kb * SSM_KB_CH:(kb + 1) * SSM_KB_CH], bblk_ref[kb])

    for kb in range(SSM_KB):
        for j in range(SSM_KB_ST // SSM_SLAB):
            re0 = kb * 2 * SSM_KB_ST + j * SSM_SLAB
            im0 = re0 + SSM_KB_ST
            co = kb * SSM_KB_ST + j * SSM_SLAB
            ar = ar_ref[:, co:co + SSM_SLAB]
            ai = ai_ref[:, co:co + SSM_SLAB]

            def body(t, carry, re0=re0, im0=im0, ar=ar, ai=ai):
                xr, xi = carry
                r0 = pl.multiple_of(t * bsz, bsz)
                br = xs_ref[pl.ds(r0, bsz), re0:re0 + SSM_SLAB]
                bi = xs_ref[pl.ds(r0, bsz), im0:im0 + SSM_SLAB]
                nr = ar * xr - ai * xi + br
                ni = ar * xi + ai * xr + bi
                xs_ref[pl.ds(r0, bsz), re0:re0 + SSM_SLAB] = nr
                xs_ref[pl.ds(r0, bsz), im0:im0 + SSM_SLAB] = ni
                return nr, ni

            xr, xi = lax.fori_loop(0, tc, body,
                                   (st_ref[:, re0:re0 + SSM_SLAB], st_ref[:, im0:im0 + SSM_SLAB]),
                                   unroll=8)
            st_ref[:, re0:re0 + SSM_SLAB] = xr
            st_ref[:, im0:im0 + SSM_SLAB] = xi

    ys = []
    for kb in range(SSM_KB):
        xb = xs_ref[:, kb * 2 * SSM_KB_ST:(kb + 1) * 2 * SSM_KB_ST].astype(BF16)
        ys.append(_dot(xb, cblk_ref[kb]))
    y = jnp.concatenate(ys, axis=-1) + d_ref[...] * u
    yg = _gelu(y)
    o_ref[...] = yg * _sigmoid(_dot(yg.astype(BF16), wglu_ref[...]))


def _ssm(u2, bblk, cblk, ar, ai, d, wglu, bsz, seq, tc):
    rows = tc * bsz
    full2 = lambda c: (0, 0)
    full3 = lambda c: (0, 0, 0)
    return pl.pallas_call(
        functools.partial(_ssm_kernel, bsz=bsz, tc=tc),
        grid=(seq // tc,),
        in_specs=[pl.BlockSpec((rows, SSM_WIDTH), lambda c: (c, 0)),
                  pl.BlockSpec(bblk.shape, full3),
                  pl.BlockSpec(cblk.shape, full3),
                  pl.BlockSpec(ar.shape, full2),
                  pl.BlockSpec(ai.shape, full2),
                  pl.BlockSpec((1, SSM_WIDTH), full2),
                  pl.BlockSpec((SSM_WIDTH, SSM_WIDTH), full2)],
        out_specs=pl.BlockSpec((rows, SSM_WIDTH), lambda c: (c, 0)),
        out_shape=jax.ShapeDtypeStruct((seq * bsz, SSM_WIDTH), F32),
        scratch_shapes=[pltpu.VMEM((rows, SSM_KB * 2 * SSM_KB_ST), F32),
                        pltpu.VMEM((bsz, SSM_KB * 2 * SSM_KB_ST), F32)],
        compiler_params=pltpu.CompilerParams(dimension_semantics=("arbitrary",),
                                             vmem_limit_bytes=VMEM_LIMIT),
        name="ssm",
    )(u2, bblk, cblk, ar, ai, d, wglu)


def _ssm_params(a_re, a_im, log_dt, b_re, b_im, c_re, c_im, bsz):
    ar, ai = a_re.astype(F32), a_im.astype(F32)
    dt = jnp.exp(log_dt.astype(F32))[:, None]
    decay = jnp.exp(dt * ar)
    abar_r, abar_i = decay * jnp.cos(dt * ai), decay * jnp.sin(dt * ai)
    den = ar * ar + ai * ai
    nr, ni = abar_r - 1.0, abar_i
    fr, fi = (nr * ar + ni * ai) / den, (ni * ar - nr * ai) / den
    br, bi = b_re.astype(F32), b_im.astype(F32)
    bbar_r = fr[..., None] * br - fi[..., None] * bi
    bbar_i = fr[..., None] * bi + fi[..., None] * br
    gl = SSM_GROUPS // SSM_KB
    eye = jnp.eye(gl, dtype=F32)

    def pack_b(bb):
        bb = bb.reshape(SSM_KB, gl, SSM_STATE, SSM_GROUP)
        dense = jnp.einsum('kgph,gq->kghqp', bb, eye)
        return dense.reshape(SSM_KB, gl * SSM_GROUP, gl * SSM_STATE)

    def pack_c(cc):
        cc = cc.reshape(SSM_KB, gl, SSM_GROUP, SSM_STATE)
        dense = jnp.einsum('kghp,gq->kgpqh', cc, eye)
        return dense.reshape(SSM_KB, gl * SSM_STATE, gl * SSM_GROUP)

    bblk = jnp.concatenate([pack_b(bbar_r), pack_b(bbar_i)], axis=-1).astype(BF16)
    cblk = jnp.concatenate([pack_c(c_re.astype(F32)), pack_c(-c_im.astype(F32))], axis=1).astype(BF16)
    ar_b = jnp.broadcast_to(abar_r.reshape(1, -1), (bsz, SSM_GROUPS * SSM_STATE))
    ai_b = jnp.broadcast_to(abar_i.reshape(1, -1), (bsz, SSM_GROUPS * SSM_STATE))
    return bblk, cblk, ar_b, ai_b


def _mlstm_kernel(qk_ref, v_ref, o_ref, gif_ref, convw_ref, bias_ref, gain_ref, out_ref,
                  xp_ref, ct_ref, n_ref, m_ref):
    lc, dh, nh = MLSTM_CHUNK, MLSTM_HEAD_DIM, MLSTM_HEADS
    halo = SUBLANES

    @pl.when(pl.program_id(1) == 0)
    def _():
        xp_ref[0:halo, :] = jnp.zeros((halo, 2 * MLSTM_WIDTH), F32)
        ct_ref[...] = jnp.zeros_like(ct_ref)
        n_ref[...] = jnp.zeros_like(n_ref)
        m_ref[...] = jnp.zeros_like(m_ref)

    xp_ref[halo:halo + lc, :] = qk_ref[...]
    conv = convw_ref[CONV_WIDTH - 1:CONV_WIDTH, :] * xp_ref[halo:halo + lc, :]
    for j in range(CONV_WIDTH - 1):
        off = halo - (CONV_WIDTH - 1) + j
        conv = conv + convw_ref[j:j + 1, :] * xp_ref[off:off + lc, :]
    xp_ref[0:halo, :] = xp_ref[lc:lc + halo, :]
    qk = conv * _sigmoid(conv)

    gpre = gif_ref[...] + bias_ref[...]
    lsig = jnp.minimum(gpre, 0.0) - jnp.log(1.0 + jnp.exp(-jnp.abs(gpre)))
    row_i = lax.broadcasted_iota(jnp.int32, (lc, lc), 0)
    col_i = lax.broadcasted_iota(jnp.int32, (lc, lc), 1)
    causal = col_i <= row_i
    tri = jnp.where(causal, 1.0, 0.0).astype(BF16)
    tri_t = jnp.where(row_i <= col_i, 1.0, 0.0).astype(BF16)
    l_hi, l_mid, l_lo = _split3(lsig)
    bcum_c = _dot(tri, l_hi) + _dot(tri, l_mid) + _dot(tri, l_lo)
    lsig_t = lsig.T
    t_hi, t_mid, t_lo = _split3(lsig_t)
    bcum_r = _dot(t_hi, tri_t) + _dot(t_mid, tri_t) + _dot(t_lo, tri_t)
    gpre_t = gpre.T

    for h in range(nh):
        qf = qk[:, h * dh:(h + 1) * dh]
        q = qf.astype(BF16)
        k = (qk[:, MLSTM_WIDTH + h * dh:MLSTM_WIDTH + (h + 1) * dh] * (dh ** -0.5))
        kb = k.astype(BF16)
        v = v_ref[:, h * dh:(h + 1) * dh]
        vb = v.astype(BF16)
        bc = bcum_c[:, nh + h:nh + h + 1]
        brow = bcum_r[nh + h:nh + h + 1, :]
        ic = gpre[:, h:h + 1]
        irow = gpre_t[h:h + 1, :]
        gtot = bc[lc - 1:lc, :]
        m_prev = m_ref[h:h + 1, 0:1]
        n_prev = n_ref[h:h + 1, :]
        ct_prev = ct_ref[h]

        log_d = jnp.where(causal, bc - brow + irow, NEG_INF)
        m_intra = jnp.max(log_d, axis=-1, keepdims=True)
        log_inter = bc + m_prev
        m_q = jnp.maximum(log_inter, m_intra)
        s_mat = _dot_nt(q, kb) * jnp.exp(log_d - m_q)
        inter = jnp.exp(log_inter - m_q)
        num =_dot(s_mat.astype(BF16), vb) + inter * _dot(q, ct_prev.astype(BF16))
        den = jnp.sum(s_mat, axis=-1, keepdims=True) + inter * jnp.sum(qf * n_prev, axis=-1, keepdims=True)
        hh = num / jnp.maximum(jnp.abs(den), jnp.exp(-m_q))
        hh = hh * lax.rsqrt(jnp.mean(hh * hh, axis=-1, keepdims=True) + RMS_EPS)
        og = _sigmoid(o_ref[:, h * dh:(h + 1) * dh])
        out_ref[:, h * dh:(h + 1) * dh] = og * (hh * gain_ref[:, h * dh:(h + 1) * dh])

        m_loc = jnp.max(gtot - brow + irow, axis=-1, keepdims=True)
        wgt_c = jnp.exp(gtot - bc + ic - m_loc)
        d_ct = _dot(k.T.astype(BF16), (wgt_c * v).astype(BF16))
        d_n = jnp.sum(wgt_c * k, axis=0, keepdims=True)
        m_new = jnp.maximum(gtot + m_prev, m_loc)
        a = jnp.exp(gtot + m_prev - m_new)
        bb = jnp.exp(m_loc - m_new)
        ct_ref[h] = a * ct_prev + bb * d_ct
        n_ref[h:h + 1, :] = a * n_prev + bb * d_n
        m_ref[h:h + 1, :] = jnp.broadcast_to(m_new, (1, LANES))


def _mlstm(qk, mv, mo, gif, convw, bias, gain, bsz, seq):
    t = bsz * seq
    lc = MLSTM_CHUNK
    nc = seq // lc
    row = lambda b, c: (b * nc + c, 0)
    full = lambda b, c: (0, 0)
    return pl.pallas_call(
        _mlstm_kernel,
        grid=(bsz, nc),
        in_specs=[pl.BlockSpec((lc, 2 * MLSTM_WIDTH), row),
                  pl.BlockSpec((lc, MLSTM_WIDTH), row),
                  pl.BlockSpec((lc, MLSTM_WIDTH), row),
                  pl.BlockSpec((lc, LANES), row),
                  pl.BlockSpec((CONV_WIDTH, 2 * MLSTM_WIDTH), full),
                  pl.BlockSpec((1, LANES), full),
                  pl.BlockSpec((1, MLSTM_WIDTH), full)],
        out_specs=pl.BlockSpec((lc, MLSTM_WIDTH), row),
        out_shape=jax.ShapeDtypeStruct((t, MLSTM_WIDTH), F32),
        scratch_shapes=[pltpu.VMEM((lc + SUBLANES, 2 * MLSTM_WIDTH), F32),
                        pltpu.VMEM((MLSTM_HEADS, MLSTM_HEAD_DIM, MLSTM_HEAD_DIM), F32),
                        pltpu.VMEM((SUBLANES, MLSTM_HEAD_DIM), F32),
                        pltpu.VMEM((SUBLANES, LANES), F32)],
        compiler_params=pltpu.CompilerParams(dimension_semantics=("parallel", "arbitrary"),
                                             vmem_limit_bytes=VMEM_LIMIT),
        name="mlstm",
    )(qk, mv, mo, gif, convw, bias, gain)


def _rel_bucket(dist):
    max_exact = REL_BUCKETS // 2
    is_small = dist < max_exact
    large = max_exact + (jnp.log(jnp.maximum(dist, 1).astype(F32) / max_exact)
                         / math.log(REL_MAX_DIST / max_exact) * (REL_BUCKETS - max_exact)).astype(jnp.int32)
    large = jnp.minimum(large, REL_BUCKETS - 1)
    return jnp.where(is_small, dist, large)


LOG2E = math.log2(math.e)
MOBA_LOOKAHEAD = 8


def _moba_bias_tiles(rel_bias):
    bs = MOBA_BLOCK
    d_own = jnp.arange(bs)[None, :] - jnp.arange(bs)[:, None]
    tab = rel_bias.astype(F32) * LOG2E
    buckets = jnp.stack([_rel_bucket(jnp.maximum(d_own, 0)), _rel_bucket(bs + d_own)])
    onehot = (buckets[..., None] == jnp.arange(REL_BUCKETS)).astype(F32)
    tiles = jnp.einsum('tjib,hb->htji', onehot, tab, precision=lax.Precision.HIGHEST)
    causal = jnp.stack([d_own >= 0, jnp.ones_like(d_own, dtype=bool)])
    tiles = jnp.where(causal[None], tiles, NEG_INF)
    far = jnp.take(tab, _rel_bucket(jnp.full((1,), 2 * bs, jnp.int32)), axis=1)
    far_rows = jnp.broadcast_to(far[:, None, :], (MOBA_HEADS, SUBLANES, LANES)).reshape(-1, LANES)
    return tiles, jnp.pad(far_rows, ((0, LANES - MOBA_HEADS * SUBLANES), (0, 0)))


def _moba_kernel(q_ref, k_ref, v_ref, tiles_ref, far_ref, out_ref,
                 kaug_ref, vt_ref, km_ref, qaug_ref, *, nb):
    bs, nh, dh = MOBA_BLOCK, MOBA_HEADS, MOBA_HEAD_DIM
    qi = pl.program_id(1)

    @pl.when(qi == 0)
    def _():
        lane = lax.broadcasted_iota(jnp.int32, (bs, LANES), 1)
        means = []
        for n in range(nb):
            kblk = k_ref[n * bs:(n + 1) * bs, :]
            means.append(jnp.mean(kblk, axis=0, keepdims=True))
            for h in range(nh):
                p, e = divmod(h, 2)
                o = (1 - e) * dh
                hot = jnp.where(lane == o + n, 1.0, jnp.where(lane == o + SUBLANES + n, 1.0, 0.0))
                kaug_ref[h, n] = jnp.where(lane // dh == e, kblk[:, p * LANES:(p + 1) * LANES], hot).astype(BF16)
            vt_ref[n] = v_ref[n * bs:(n + 1) * bs, :].T.astype(BF16)
        means += [jnp.zeros_like(means[0])] * (SUBLANES - nb)
        kmean = jnp.concatenate(means, axis=0)
        rows = lax.broadcasted_iota(jnp.int32, (LANES, MOBA_WIDTH), 0)
        cols = lax.broadcasted_iota(jnp.int32, (LANES, MOBA_WIDTH), 1)
        tiled = jnp.concatenate([kmean] * (LANES // SUBLANES), axis=0)
        km_ref[...] = jnp.where(rows // SUBLANES == cols // dh, tiled, 0.0)

    q = q_ref[...]
    q_hi, q_mid, q_lo = _split3(q)
    m_hi, m_mid, m_lo = _split3(km_ref[...])
    gate = (_dot_nt(m_hi, q_hi) + _dot_nt(m_mid, q_hi) + _dot_nt(m_hi, q_mid)
            + _dot_nt(m_lo, q_hi) + _dot_nt(m_hi, q_lo) + _dot_nt(m_mid, q_mid))
    nrows = LANES
    blk = lax.broadcasted_iota(jnp.int32, (nrows, bs), 0) % SUBLANES
    gate = jnp.where(blk < qi, gate, NEG_INF)
    rank = jnp.zeros((nrows, bs), F32)
    for s in range(1, SUBLANES):
        fwd = pltpu.roll(gate, nrows - s, 0)
        back = pltpu.roll(gate, SUBLANES - s, 0)
        wrapped = blk + s >= SUBLANES
        other = jnp.where(wrapped, back, fwd)
        rank = rank + jnp.where(wrapped, jnp.where(other >= gate, 1.0, 0.0), jnp.where(other > gate, 1.0, 0.0))
    chosen = rank < MOBA_TOPK
    far = jnp.concatenate([far_ref[...]] * (bs // LANES), axis=1)
    far_hi = far.astype(BF16).astype(F32)
    far_lo = far - far_hi
    sel_hi = jnp.where(blk < qi - 1, jnp.where(chosen, far_hi, NEG_INF),
                       jnp.where(blk == qi - 1, jnp.where(chosen, 0.0, NEG_INF),
                                 jnp.where(blk == qi, 0.0, NEG_INF)))
    sel_lo = jnp.where(blk < qi - 1, jnp.where(chosen, far_lo, 0.0), 0.0)

    q_t = (q * (dh ** -0.5 * LOG2E)).T
    pad = jnp.zeros((dh - 2 * SUBLANES, bs), F32)
    for h in range(nh):
        e = h % 2
        extra = [sel_hi[h * SUBLANES:(h + 1) * SUBLANES], sel_lo[h * SUBLANES:(h + 1) * SUBLANES], pad]
        q_h = [q_t[h * dh:(h + 1) * dh]]
        qaug_ref[h] = jnp.concatenate(q_h + extra if e == 0 else extra + q_h, axis=0).astype(BF16)

    def scores(h, n):
        return _dot(kaug_ref[h, n], qaug_ref[h])

    def absorb(h, n, s, state):
        mx = jnp.max(s, axis=0, keepdims=True)
        v_t = vt_ref[n, h * dh:(h + 1) * dh, :]
        m_old, l_old, acc = state
        m_new = jnp.maximum(m_old, mx)
        alpha = jnp.exp2(m_old - m_new)
        pr = jnp.exp2(s - m_new)
        return (m_new, alpha * l_old + jnp.sum(pr, axis=0, keepdims=True),
                alpha * acc + _dot(v_t, pr.astype(BF16)))

    def run_tiles(jobs, states):
        states = list(states)
        pending = {}
        for i in range(min(MOBA_LOOKAHEAD, len(jobs))):
            pending[i] = scores(*jobs[i][:2])
        for i, (h, n, tile) in enumerate(jobs):
            s = pending.pop(i)
            if tile is not None:
                s = tile + s
            states[h] = absorb(h, n, s, states[h])
            if i + MOBA_LOOKAHEAD < len(jobs):
                pending[i + MOBA_LOOKAHEAD] = scores(*jobs[i + MOBA_LOOKAHEAD][:2])
        return tuple(states)

    states = tuple((jnp.full((1, bs), NEG_INF, F32), jnp.zeros((1, bs), F32), jnp.zeros((dh, bs), F32))
                   for _ in range(nh))

    def near_body(k, carry):
        return run_tiles([(h, qi - k, tiles_ref[h, k]) for h in range(nh)], carry)

    def far_body(n, carry):
        return run_tiles([(h, n, None) for h in range(nh)], carry)

    states = lax.fori_loop(0, jnp.minimum(qi + 1, 2), near_body, states)
    states = lax.fori_loop(0, jnp.maximum(qi - 1, 0), far_body, states)
    out_t = jnp.concatenate([acc * (1.0 / l) for (_, l, acc) in states], axis=0)
    out_ref[...] = out_t.T


def _moba(aq, ak, av, tiles, far, bsz, seq):
    t = bsz * seq
    bs = MOBA_BLOCK
    nb = seq // bs
    return pl.pallas_call(
        functools.partial(_moba_kernel, nb=nb),
        grid=(bsz, nb),
        in_specs=[pl.BlockSpec((bs, MOBA_WIDTH), lambda b, i: (b * nb + i, 0)),
                  pl.BlockSpec((seq, MOBA_WIDTH), lambda b, i: (b, 0)),
                  pl.BlockSpec((seq, MOBA_WIDTH), lambda b, i: (b, 0)),
                  pl.BlockSpec(tiles.shape, lambda b, i: (0, 0, 0, 0)),
                  pl.BlockSpec(far.shape, lambda b, i: (0, 0))],
        out_specs=pl.BlockSpec((bs, MOBA_WIDTH), lambda b, i: (b * nb + i, 0)),
        out_shape=jax.ShapeDtypeStruct((t, MOBA_WIDTH), F32),
        scratch_shapes=[pltpu.VMEM((MOBA_HEADS, nb, bs, LANES), BF16),
                        pltpu.VMEM((nb, MOBA_WIDTH, bs), BF16),
                        pltpu.VMEM((LANES, MOBA_WIDTH), F32),
                        pltpu.VMEM((MOBA_HEADS, LANES, bs), BF16)],
        compiler_params=pltpu.CompilerParams(dimension_semantics=("parallel", "arbitrary"),
                                             vmem_limit_bytes=VMEM_LIMIT),
        name="moba",
    )(aq, ak, av, tiles, far)


def _merge_kernel(x_ref, ys_ref, ym_ref, ya_ref, gate_ref, wp_ref, wo_ref, g_ref, o_ref):
    merged = (gate_ref[:, 0:D_MODEL] * _dot(ys_ref[...].astype(BF16), wp_ref[0])
              + gate_ref[:, D_MODEL:2 * D_MODEL] * _dot(ym_ref[...].astype(BF16), wp_ref[1])
              + gate_ref[:, 2 * D_MODEL:3 * D_MODEL] * _dot(ya_ref[...].astype(BF16), wp_ref[2]))
    z = _dot(merged.astype(BF16), wo_ref[...])
    o_ref[...] = x_ref[...] + _rms(z, g_ref[...])


def _merge(x2d, ys_t, ym, ya, gate, wp, wo, g, bsz, seq, ts):
    t = bsz * seq
    ns = seq // ts
    row = lambda b, s: (b * ns + s, 0)
    full = lambda b, s: (0, 0)
    return pl.pallas_call(
        _merge_kernel,
        grid=(bsz, ns),
        in_specs=[pl.BlockSpec((ts, D_MODEL), row),
                  pl.BlockSpec((ts, SSM_WIDTH), lambda b, s: (s, b)),
                  pl.BlockSpec((ts, MLSTM_WIDTH), row),
                  pl.BlockSpec((ts, MOBA_WIDTH), row),
                  pl.BlockSpec((ts, 3 * D_MODEL), row),
                  pl.BlockSpec(wp.shape, lambda b, s: (0, 0, 0)),
                  pl.BlockSpec(wo.shape, full),
                  pl.BlockSpec((1, D_MODEL), full)],
        out_specs=pl.BlockSpec((ts, D_MODEL), row),
        out_shape=jax.ShapeDtypeStruct((t, D_MODEL), F32),
        compiler_params=pltpu.CompilerParams(dimension_semantics=("parallel", "parallel"),
                                             vmem_limit_bytes=VMEM_LIMIT),
        name="merge",
    )(x2d, ys_t, ym, ya, gate, wp, wo, g)


def _ffn_kernel(x_ref, g1_ref, w1_ref, w2_ref, g2_ref, o_ref):
    x = x_ref[...]
    hb = _rms(x, g1_ref[...]).astype(BF16)
    a = jnp.maximum(_dot(hb, w1_ref[...]), 0.0)
    f = _dot((a * a).astype(BF16), w2_ref[...])
    o_ref[...] = x + _rms(f, g2_ref[...])


def _ffn(x2d, g1, w1, w2, g2, ts):
    t = x2d.shape[0]
    full = lambda i: (0, 0)
    return pl.pallas_call(
        _ffn_kernel,
        grid=(t // ts,),
        in_specs=[pl.BlockSpec((ts, D_MODEL), lambda i: (i, 0)),
                  pl.BlockSpec((1, D_MODEL), full),
                  pl.BlockSpec((D_MODEL, D_FF), full, pipeline_mode=pl.Buffered(1)),
                  pl.BlockSpec((D_FF, D_MODEL), full, pipeline_mode=pl.Buffered(1)),
                  pl.BlockSpec((1, D_MODEL), full)],
        out_specs=pl.BlockSpec((ts, D_MODEL), lambda i: (i, 0)),
        out_shape=jax.ShapeDtypeStruct((t, D_MODEL), F32),
        compiler_params=pltpu.CompilerParams(dimension_semantics=("parallel",),
                                             vmem_limit_bytes=VMEM_LIMIT),
        name="ffn",
    )(x2d, g1, w1, w2, g2)


def _reorder_w_in(w_in):
    w = w_in.astype(BF16)
    n_if = 2 * MLSTM_HEADS
    c_if = SSM_WIDTH + 4 * MLSTM_WIDTH
    gif = jnp.pad(w[..., c_if:c_if + n_if], ((0, 0), (0, 0), (0, LANES - n_if)))
    return jnp.concatenate([w[..., :c_if], w[..., c_if + n_if:], gif], axis=-1)


def kernel(x, w_in, conv_w, ssm_a_re, ssm_a_im, ssm_log_dt, ssm_b_re, ssm_b_im, ssm_c_re, ssm_c_im, ssm_d, ssm_w_glu, mlstm_i_bias, mlstm_f_bias, mlstm_head_gain, rel_bias, w_ssm_proj, w_mlstm_proj, w_moba_proj, w_out, w_ff1, w_ff2, norm_mix_pre, norm_mix_post, norm_ffn_pre, norm_ffn_post):
    bsz, seq, _ = x.shape
    depth = w_in.shape[0]
    assert bsz == SUBLANES and seq % MOBA_BLOCK == 0 and seq // MOBA_BLOCK <= SUBLANES
    ts = 256
    tc = 64
    x2d = x.reshape(bsz * seq, D_MODEL)
    w_in_r = _reorder_w_in(w_in)
    wp = jnp.stack([w_ssm_proj, w_mlstm_proj, w_moba_proj], axis=1).astype(BF16)
    wo = w_out.astype(BF16)
    w1 = w_ff1.astype(BF16)
    w2 = w_ff2.astype(BF16)
    wglu = ssm_w_glu.astype(BF16)
    gate_bias = jnp.pad(jnp.concatenate([mlstm_i_bias, mlstm_f_bias], axis=-1).astype(F32),
                        ((0, 0), (0, LANES - 2 * MLSTM_HEADS)))
    tiles, far = _moba_bias_tiles(rel_bias)
    for l in range(depth):
        u_t, qk, mv, mo, aq, ak, av, gate, gif = _inproj(
            x2d, norm_mix_pre[l][None, :], w_in_r[l], bsz, seq, ts)
        bblk, cblk, ar_b, ai_b = _ssm_params(ssm_a_re[l], ssm_a_im[l], ssm_log_dt[l], ssm_b_re[l], ssm_b_im[l],
                                             ssm_c_re[l], ssm_c_im[l], bsz)
        y_ssm = _ssm(u_t.reshape(seq * bsz, SSM_WIDTH), bblk, cblk, ar_b, ai_b, ssm_d[l][None, :], wglu[l],
                     bsz, seq, tc)
        y_mlstm = _mlstm(qk, mv, mo, gif, conv_w[l].astype(F32), gate_bias[l][None, :],
                         mlstm_head_gain[l][None, :].astype(F32), bsz, seq)
        y_moba = _moba(aq, ak, av, tiles, far, bsz, seq)
        x2d = _merge(x2d, y_ssm.reshape(seq, bsz * SSM_WIDTH), y_mlstm, y_moba, gate, wp[l], wo[l],
                     norm_mix_post[l][None, :], bsz, seq, ts)
        x2d = _ffn(x2d, norm_ffn_pre[l][None, :], w1[l], w2[l], norm_ffn_post[l][None, :], ts)
    return x2d.reshape(bsz, seq, D_MODEL)
```

```python
import functools
import math

import jax
import jax.numpy as jnp
from jax import lax
from jax.experimental import pallas as pl
from jax.experimental.pallas import tpu as pltpu

F32 = jnp.float32
BF16 = jnp.bfloat16

D_MODEL = 1024
SSM_WIDTH = 512
SSM_GROUP = 16
SSM_GROUPS = 32
SSM_STATE = 64
MLSTM_WIDTH = 512
MLSTM_HEADS = 4
MLSTM_HEAD_DIM = 128
MLSTM_CHUNK = 128
CONV_WIDTH = 4
MOBA_WIDTH = 512
MOBA_HEADS = 8
MOBA_HEAD_DIM = 64
MOBA_BLOCK = 256
MOBA_TOPK = 3
REL_BUCKETS = 32
REL_MAX_DIST = 128
D_FF = 4 * D_MODEL
RMS_EPS = 1e-6
NEG_INF = -1e30

LANES = 128
SUBLANES = 8
VMEM_LIMIT = 56 * 1024 * 1024

C_U = 0
C_QK = 512
C_MV = 1536
C_MO = 2048
C_AQ = 2560
C_AK = 3072
C_AV = 3584
C_GATE = 4096
C_GIF = 7168
C_END = 7296


def _rms(x, g):
    return x * lax.rsqrt(jnp.mean(x * x, axis=-1, keepdims=True) + RMS_EPS) * g


def _sigmoid(x):
    return 1.0 / (1.0 + jnp.exp(-x))


def _dot(a, b):
    return jnp.dot(a, b, preferred_element_type=F32)


def _dot_nt(a, b):
    return lax.dot_general(a, b, (((1,), (1,)), ((), ())), preferred_element_type=F32)


def _split3(a):
    hi = a.astype(BF16)
    r1 = a - hi.astype(F32)
    mid = r1.astype(BF16)
    lo = (r1 - mid.astype(F32)).astype(BF16)
    return hi, mid, lo


def _inproj_kernel(x_ref, g_ref, w_ref, u_ref, qk_ref, mv_ref, mo_ref, aq_ref, ak_ref, av_ref,
                   gate_ref, gif_ref):
    hb = _rms(x_ref[...], g_ref[...]).astype(BF16)

    def proj(a, b):
        return _dot(hb, w_ref[:, a:b])

    u_ref[...] = proj(C_U, C_QK)
    qk_ref[...] = proj(C_QK, C_MV)
    mv_ref[...] = proj(C_MV, C_MO)
    mo_ref[...] = proj(C_MO, C_AQ)
    aq_ref[...] = proj(C_AQ, C_AK)
    ak_ref[...] = proj(C_AK, C_AV)
    av_ref[...] = proj(C_AV, C_GATE)
    gate_ref[...] = _sigmoid(proj(C_GATE, C_GIF)).astype(BF16)
    gif_ref[...] = proj(C_GIF, C_END)


def _inproj(x2d, g, w, ts):
    t = x2d.shape[0]
    row = lambda i: (i, 0)
    full = lambda i: (0, 0)

    def out(n, dtype=F32):
        return jax.ShapeDtypeStruct((t, n), dtype), pl.BlockSpec((ts, n), row)

    outs = [out(512), out(1024), out(512), out(512), out(512), out(512), out(512), out(3 * D_MODEL, BF16),
            out(LANES)]
    return pl.pallas_call(
        _inproj_kernel,
        grid=(t // ts,),
        in_specs=[pl.BlockSpec((ts, D_MODEL), row),
                  pl.BlockSpec((1, D_MODEL), full),
                  pl.BlockSpec((D_MODEL, C_END), full, pipeline_mode=pl.Buffered(1))],
        out_specs=[o[1] for o in outs],
        out_shape=[o[0] for o in outs],
        compiler_params=pltpu.CompilerParams(dimension_semantics=("parallel",),
                                             vmem_limit_bytes=VMEM_LIMIT),
        name="inproj",
    )(x2d, g, w)


SSM_KB = 2
SSM_KB_CH = SSM_WIDTH // SSM_KB
SSM_KB_ST = SSM_KB_CH // SSM_GROUP * SSM_STATE
SSM_SLAB = 512


def _gelu(x):
    return 0.5 * x * (1.0 + jnp.tanh(math.sqrt(2.0 / math.pi) * (x + 0.044715 * (x * x * x))))


def _ssm_kernel(u_ref, perm_ref, perm_t_ref, bblk_ref, cblk_ref, ar_ref, ai_ref, d_ref, wglu_ref, o_ref,
                xs_ref, st_ref, *, bsz, tc):
    @pl.when(pl.program_id(0) == 0)
    def _():
        st_ref[...] = jnp.zeros_like(st_ref)

    u_bm = u_ref[...].reshape(bsz * tc, SSM_WIDTH)
    u_hi = u_bm.astype(BF16)
    u_lo = (u_bm - u_hi.astype(F32)).astype(BF16)
    u_tm_hi = _dot(perm_ref[...], u_hi)
    u = u_tm_hi + _dot(perm_ref[...], u_lo)
    ub = u_tm_hi.astype(BF16)
    for kb in range(SSM_KB):
        xs_ref[:, kb * 2 * SSM_KB_ST:(kb + 1) * 2 * SSM_KB_ST] = _dot(
            ub[:, kb * SSM_KB_CH:(kb + 1) * SSM_KB_CH], bblk_ref[kb])

    for kb in range(SSM_KB):
        for j in range(SSM_KB_ST // SSM_SLAB):
            re0 = kb * 2 * SSM_KB_ST + j * SSM_SLAB
            im0 = re0 + SSM_KB_ST
            co = kb * SSM_KB_ST + j * SSM_SLAB
            ar = ar_ref[:, co:co + SSM_SLAB]
            ai = ai_ref[:, co:co + SSM_SLAB]

            def body(t, carry, re0=re0, im0=im0, ar=ar, ai=ai):
                xr, xi = carry
                r0 = pl.multiple_of(t * bsz, bsz)
                br = xs_ref[pl.ds(r0, bsz), re0:re0 + SSM_SLAB]
                bi = xs_ref[pl.ds(r0, bsz), im0:im0 + SSM_SLAB]
                nr = ar * xr - ai * xi + br
                ni = ar * xi + ai * xr + bi
                xs_ref[pl.ds(r0, bsz), re0:re0 + SSM_SLAB] = nr
                xs_ref[pl.ds(r0, bsz), im0:im0 + SSM_SLAB] = ni
                return nr, ni

            xr, xi = lax.fori_loop(0, tc, body,
                                   (st_ref[:, re0:re0 + SSM_SLAB], st_ref[:, im0:im0 + SSM_SLAB]),
                                   unroll=8)
            st_ref[:, re0:re0 + SSM_SLAB] = xr
            st_ref[:, im0:im0 + SSM_SLAB] = xi

    ys = []
    for kb in range(SSM_KB):
        xb = xs_ref[:, kb * 2 * SSM_KB_ST:(kb + 1) * 2 * SSM_KB_ST].astype(BF16)
        ys.append(_dot(xb, cblk_ref[kb]))
    y = jnp.concatenate(ys, axis=-1) + d_ref[...] * u
    yg = _gelu(y)
    out_tm = (yg * _sigmoid(_dot(yg.astype(BF16), wglu_ref[...]))).astype(BF16)
    o_ref[...] = _dot(perm_t_ref[...], out_tm).astype(BF16).reshape(bsz, tc, SSM_WIDTH)


def _ssm(u3, bblk, cblk, ar, ai, d, wglu, bsz, seq, tc):
    rows = tc * bsz
    full2 = lambda c: (0, 0)
    full3 = lambda c: (0, 0, 0)
    r = jnp.arange(rows)
    perm = (r[None, :] == (r[:, None] % bsz) * tc + r[:, None] // bsz).astype(BF16)
    return pl.pallas_call(
        functools.partial(_ssm_kernel, bsz=bsz, tc=tc),
        grid=(seq // tc,),
        in_specs=[pl.BlockSpec((bsz, tc, SSM_WIDTH), lambda c: (0, c, 0)),
                  pl.BlockSpec((rows, rows), full2),
                  pl.BlockSpec((rows, rows), full2),
                  pl.BlockSpec(bblk.shape, full3),
                  pl.BlockSpec(cblk.shape, full3),
                  pl.BlockSpec(ar.shape, full2),
                  pl.BlockSpec(ai.shape, full2),
                  pl.BlockSpec((1, SSM_WIDTH), full2),
                  pl.BlockSpec((SSM_WIDTH, SSM_WIDTH), full2)],
        out_specs=pl.BlockSpec((bsz, tc, SSM_WIDTH), lambda c: (0, c, 0)),
        out_shape=jax.ShapeDtypeStruct((bsz, seq, SSM_WIDTH), BF16),
        scratch_shapes=[pltpu.VMEM((rows, SSM_KB * 2 * SSM_KB_ST), F32),
                        pltpu.VMEM((bsz, SSM_KB * 2 * SSM_KB_ST), F32)],
        compiler_params=pltpu.CompilerParams(dimension_semantics=("arbitrary",),
                                             vmem_limit_bytes=VMEM_LIMIT),
        name="ssm",
    )(u3, perm, perm.T, bblk, cblk, ar, ai, d, wglu)


def _ssm_params(a_re, a_im, log_dt, b_re, b_im, c_re, c_im, bsz):
    ar, ai = a_re.astype(F32), a_im.astype(F32)
    dt = jnp.exp(log_dt.astype(F32))[:, None]
    decay = jnp.exp(dt * ar)
    abar_r, abar_i = decay * jnp.cos(dt * ai), decay * jnp.sin(dt * ai)
    den = ar * ar + ai * ai
    nr, ni = abar_r - 1.0, abar_i
    fr, fi = (nr * ar + ni * ai) / den, (ni * ar - nr * ai) / den
    br, bi = b_re.astype(F32), b_im.astype(F32)
    bbar_r = fr[..., None] * br - fi[..., None] * bi
    bbar_i = fr[..., None] * bi + fi[..., None] * br
    gl = SSM_GROUPS // SSM_KB
    eye = jnp.eye(gl, dtype=F32)

    def pack_b(bb):
        bb = bb.reshape(SSM_KB, gl, SSM_STATE, SSM_GROUP)
        dense = jnp.einsum('kgph,gq->kghqp', bb, eye)
        return dense.reshape(SSM_KB, gl * SSM_GROUP, gl * SSM_STATE)

    def pack_c(cc):
        cc = cc.reshape(SSM_KB, gl, SSM_GROUP, SSM_STATE)
        dense = jnp.einsum('kghp,gq->kgpqh', cc, eye)
        return dense.reshape(SSM_KB, gl * SSM_STATE, gl * SSM_GROUP)

    bblk = jnp.concatenate([pack_b(bbar_r), pack_b(bbar_i)], axis=-1).astype(BF16)
    cblk = jnp.concatenate([pack_c(c_re.astype(F32)), pack_c(-c_im.astype(F32))], axis=1).astype(BF16)
    ar_b = jnp.broadcast_to(abar_r.reshape(1, -1), (bsz, SSM_GROUPS * SSM_STATE))
    ai_b = jnp.broadcast_to(abar_i.reshape(1, -1), (bsz, SSM_GROUPS * SSM_STATE))
    return bblk, cblk, ar_b, ai_b


def _mlstm_kernel(qk_ref, v_ref, o_ref, gif_ref, convw_ref, bias_ref, gain_ref, out_ref,
                  xp_ref, ct_ref, n_ref, m_ref):
    lc, dh, nh = MLSTM_CHUNK, MLSTM_HEAD_DIM, MLSTM_HEADS
    halo = SUBLANES

    @pl.when(pl.program_id(1) == 0)
    def _():
        xp_ref[0:halo, :] = jnp.zeros((halo, 2 * MLSTM_WIDTH), F32)
        ct_ref[...] = jnp.zeros_like(ct_ref)
        n_ref[...] = jnp.zeros_like(n_ref)
        m_ref[...] = jnp.zeros_like(m_ref)

    xp_ref[halo:halo + lc, :] = qk_ref[...]
    conv = convw_ref[CONV_WIDTH - 1:CONV_WIDTH, :] * xp_ref[halo:halo + lc, :]
    for j in range(CONV_WIDTH - 1):
        off = halo - (CONV_WIDTH - 1) + j
        conv = conv + convw_ref[j:j + 1, :] * xp_ref[off:off + lc, :]
    xp_ref[0:halo, :] = xp_ref[lc:lc + halo, :]
    qk = conv * _sigmoid(conv)

    gpre = gif_ref[...] + bias_ref[...]
    lsig = jnp.minimum(gpre, 0.0) - jnp.log(1.0 + jnp.exp(-jnp.abs(gpre)))
    row_i = lax.broadcasted_iota(jnp.int32, (lc, lc), 0)
    col_i = lax.broadcasted_iota(jnp.int32, (lc, lc), 1)
    causal = col_i <= row_i
    tri = jnp.where(causal, 1.0, 0.0).astype(BF16)
    tri_t = jnp.where(row_i <= col_i, 1.0, 0.0).astype(BF16)
    l_hi, l_mid, l_lo = _split3(lsig)
    bcum_c = _dot(tri, l_hi) + _dot(tri, l_mid) + _dot(tri, l_lo)
    lsig_t = lsig.T
    t_hi, t_mid, t_lo = _split3(lsig_t)
    bcum_r = _dot(t_hi, tri_t) + _dot(t_mid, tri_t) + _dot(t_lo, tri_t)
    gpre_t = gpre.T
    per_time = jnp.where(col_i < nh, gpre, bcum_c)
    rep_rows = lax.broadcasted_iota(jnp.int32, (LANES, 2 * nh * LANES), 0)
    rep_cols = lax.broadcasted_iota(jnp.int32, (LANES, 2 * nh * LANES), 1)
    spread = jnp.where(rep_rows == rep_cols // LANES, 1.0, 0.0).astype(BF16)
    p_hi, p_mid, p_lo = _split3(per_time)
    rep = _dot(p_hi, spread) + _dot(p_mid, spread) + _dot(p_lo, spread)

    hs = range(nh)
    qf = [qk[:, h * dh:(h + 1) * dh] for h in hs]
    q = [x.astype(BF16) for x in qf]
    k = [qk[:, MLSTM_WIDTH + h * dh:MLSTM_WIDTH + (h + 1) * dh] * (dh ** -0.5) for h in hs]
    kb = [x.astype(BF16) for x in k]
    v = [v_ref[:, h * dh:(h + 1) * dh] for h in hs]
    bc = [rep[:, (nh + h) * LANES:(nh + h + 1) * LANES] for h in hs]
    brow = [bcum_r[nh + h:nh + h + 1, :] for h in hs]
    ic = [rep[:, h * LANES:(h + 1) * LANES] for h in hs]
    irow = [gpre_t[h:h + 1, :] for h in hs]
    gtot = [x[lc - 1:lc, :] for x in bc]
    m_prev = [m_ref[h:h + 1, :] for h in hs]
    n_prev = [n_ref[h:h + 1, :] for h in hs]
    ct_prev = [ct_ref[h] for h in hs]

    qk_t = [_dot_nt(q[h], kb[h]) for h in hs]
    q_ct = [_dot(q[h], ct_prev[h].astype(BF16)) for h in hs]
    m_loc = [jnp.max(gtot[h] - brow[h] + irow[h], axis=-1, keepdims=True) for h in hs]
    wgt_c = [jnp.exp(gtot[h] - bc[h] + ic[h] - m_loc[h]) for h in hs]
    d_ct = [_dot(k[h].T.astype(BF16), (wgt_c[h] * v[h]).astype(BF16)) for h in hs]

    log_d = [jnp.where(causal, bc[h] - brow[h] + irow[h], NEG_INF) for h in hs]
    m_intra = [jnp.max(log_d[h], axis=-1, keepdims=True) for h in hs]
    qn = [jnp.sum(qf[h] * n_prev[h], axis=-1, keepdims=True) for h in hs]
    log_inter = [bc[h] + m_prev[h] for h in hs]
    m_q = [jnp.maximum(log_inter[h], m_intra[h]) for h in hs]
    s_mat = [qk_t[h] * jnp.exp(log_d[h] - m_q[h]) for h in hs]
    s_sum = [jnp.sum(s_mat[h], axis=-1, keepdims=True) for h in hs]
    s_v = [_dot(s_mat[h].astype(BF16), v[h].astype(BF16)) for h in hs]
    inter = [jnp.exp(log_inter[h] - m_q[h]) for h in hs]
    hh = [(inter[h] * q_ct[h] + s_v[h])
          / jnp.maximum(jnp.abs(s_sum[h] + inter[h] * qn[h]), jnp.exp(-m_q[h])) for h in hs]
    ms = [jnp.mean(hh[h] * hh[h], axis=-1, keepdims=True) for h in hs]
    for h in hs:
        og = _sigmoid(o_ref[:, h * dh:(h + 1) * dh])
        out_ref[:, h * dh:(h + 1) * dh] = og * (hh[h] * lax.rsqrt(ms[h] + RMS_EPS)
                                                * gain_ref[:, h * dh:(h + 1) * dh])

    for h in hs:
        d_n = jnp.sum(wgt_c[h] * k[h], axis=0, keepdims=True)
        m_new = jnp.maximum(gtot[h] + m_prev[h], m_loc[h])
        a = jnp.exp(gtot[h] + m_prev[h] - m_new)
        bb = jnp.exp(m_loc[h] - m_new)
        ct_ref[h] = a * ct_prev[h] + bb * d_ct[h]
        n_ref[h:h + 1, :] = a * n_prev[h] + bb * d_n
        m_ref[h:h + 1, :] = m_new


def _mlstm(qk, mv, mo, gif, convw, bias, gain, bsz, seq):
    t = bsz * seq
    lc = MLSTM_CHUNK
    nc = seq // lc
    row = lambda b, c: (b * nc + c, 0)
    full = lambda b, c: (0, 0)
    return pl.pallas_call(
        _mlstm_kernel,
        grid=(bsz, nc),
        in_specs=[pl.BlockSpec((lc, 2 * MLSTM_WIDTH), row),
                  pl.BlockSpec((lc, MLSTM_WIDTH), row),
                  pl.BlockSpec((lc, MLSTM_WIDTH), row),
                  pl.BlockSpec((lc, LANES), row),
                  pl.BlockSpec((CONV_WIDTH, 2 * MLSTM_WIDTH), full),
                  pl.BlockSpec((1, LANES), full),
                  pl.BlockSpec((1, MLSTM_WIDTH), full)],
        out_specs=pl.BlockSpec((lc, MLSTM_WIDTH), row),
        out_shape=jax.ShapeDtypeStruct((t, MLSTM_WIDTH), F32),
        scratch_shapes=[pltpu.VMEM((lc + SUBLANES, 2 * MLSTM_WIDTH), F32),
                        pltpu.VMEM((MLSTM_HEADS, MLSTM_HEAD_DIM, MLSTM_HEAD_DIM), F32),
                        pltpu.VMEM((SUBLANES, MLSTM_HEAD_DIM), F32),
                        pltpu.VMEM((SUBLANES, LANES), F32)],
        compiler_params=pltpu.CompilerParams(dimension_semantics=("parallel", "arbitrary"),
                                             vmem_limit_bytes=VMEM_LIMIT),
        name="mlstm",
    )(qk, mv, mo, gif, convw, bias, gain)


def _rel_bucket(dist):
    max_exact = REL_BUCKETS // 2
    is_small = dist < max_exact
    large = max_exact + (jnp.log(jnp.maximum(dist, 1).astype(F32) / max_exact)
                         / math.log(REL_MAX_DIST / max_exact) * (REL_BUCKETS - max_exact)).astype(jnp.int32)
    large = jnp.minimum(large, REL_BUCKETS - 1)
    return jnp.where(is_small, dist, large)


LOG2E = math.log2(math.e)
MOBA_LOOKAHEAD = 8


def _moba_bias_tiles(rel_bias):
    bs = MOBA_BLOCK
    d_own = jnp.arange(bs)[None, :] - jnp.arange(bs)[:, None]
    tab = rel_bias.astype(F32) * LOG2E
    buckets = jnp.stack([_rel_bucket(jnp.maximum(d_own, 0)), _rel_bucket(bs + d_own)])
    onehot = (buckets[..., None] == jnp.arange(REL_BUCKETS)).astype(F32)
    tiles = jnp.einsum('tjib,hb->htji', onehot, tab, precision=lax.Precision.HIGHEST)
    causal = jnp.stack([d_own >= 0, jnp.ones_like(d_own, dtype=bool)])
    tiles = jnp.where(causal[None], tiles, NEG_INF)
    far = jnp.take(tab, _rel_bucket(jnp.full((1,), 2 * bs, jnp.int32)), axis=1)
    far_rows = jnp.broadcast_to(far[:, None, :], (MOBA_HEADS, SUBLANES, LANES)).reshape(-1, LANES)
    return tiles, jnp.pad(far_rows, ((0, LANES - MOBA_HEADS * SUBLANES), (0, 0)))


def _moba_kernel(q_ref, k_ref, v_ref, tiles_ref, far_ref, out_ref,
                 kaug_ref, vt_ref, km_ref, qaug_ref, *, nb):
    bs, nh, dh = MOBA_BLOCK, MOBA_HEADS, MOBA_HEAD_DIM
    qi = pl.program_id(1)

    @pl.when(qi == 0)
    def _():
        lane = lax.broadcasted_iota(jnp.int32, (bs, LANES), 1)
        means = []
        for n in range(nb):
            kblk = k_ref[n * bs:(n + 1) * bs, :]
            means.append(jnp.mean(kblk, axis=0, keepdims=True))
            for h in range(nh):
                p, e = divmod(h, 2)
                o = (1 - e) * dh
                hot = jnp.where(lane == o + n, 1.0, jnp.where(lane == o + SUBLANES + n, 1.0, 0.0))
                kaug_ref[h, n] = jnp.where(lane // dh == e, kblk[:, p * LANES:(p + 1) * LANES], hot).astype(BF16)
            vt_ref[n] = v_ref[n * bs:(n + 1) * bs, :].T.astype(BF16)
        means += [jnp.zeros_like(means[0])] * (SUBLANES - nb)
        kmean = jnp.concatenate(means, axis=0)
        rows = lax.broadcasted_iota(jnp.int32, (LANES, MOBA_WIDTH), 0)
        cols = lax.broadcasted_iota(jnp.int32, (LANES, MOBA_WIDTH), 1)
        tiled = jnp.concatenate([kmean] * (LANES // SUBLANES), axis=0)
        km_ref[...] = jnp.where(rows // SUBLANES == cols // dh, tiled, 0.0)

    q = q_ref[...]
    q_hi, q_mid, q_lo = _split3(q)
    m_hi, m_mid, m_lo = _split3(km_ref[...])
    gate = (_dot_nt(m_hi, q_hi) + _dot_nt(m_mid, q_hi) + _dot_nt(m_hi, q_mid)
            + _dot_nt(m_lo, q_hi) + _dot_nt(m_hi, q_lo) + _dot_nt(m_mid, q_mid))
    nrows = LANES
    blk = lax.broadcasted_iota(jnp.int32, (nrows, bs), 0) % SUBLANES
    gate = jnp.where(blk < qi, gate, NEG_INF)
    rank = jnp.zeros((nrows, bs), F32)
    for s in range(1, SUBLANES):
        fwd = pltpu.roll(gate, nrows - s, 0)
        back = pltpu.roll(gate, SUBLANES - s, 0)
        wrapped = blk + s >= SUBLANES
        other = jnp.where(wrapped, back, fwd)
        rank = rank + jnp.where(wrapped, jnp.where(other >= gate, 1.0, 0.0), jnp.where(other > gate, 1.0, 0.0))
    chosen = rank < MOBA_TOPK
    far = jnp.concatenate([far_ref[...]] * (bs // LANES), axis=1)
    far_hi = far.astype(BF16).astype(F32)
    far_lo = far - far_hi
    sel_hi = jnp.where(blk < qi - 1, jnp.where(chosen, far_hi, NEG_INF),
                       jnp.where(blk == qi - 1, jnp.where(chosen, 0.0, NEG_INF),
                                 jnp.where(blk == qi, 0.0, NEG_INF)))
    sel_lo = jnp.where(blk < qi - 1, jnp.where(chosen, far_lo, 0.0), 0.0)

    q_t = (q * (dh ** -0.5 * LOG2E)).T
    pad = jnp.zeros((dh - 2 * SUBLANES, bs), F32)
    for h in range(nh):
        e = h % 2
        extra = [sel_hi[h * SUBLANES:(h + 1) * SUBLANES], sel_lo[h * SUBLANES:(h + 1) * SUBLANES], pad]
        q_h = [q_t[h * dh:(h + 1) * dh]]
        qaug_ref[h] = jnp.concatenate(q_h + extra if e == 0 else extra + q_h, axis=0).astype(BF16)

    def scores(h, n):
        return _dot(kaug_ref[h, n], qaug_ref[h])

    def absorb(h, n, s, state):
        mx = jnp.max(s, axis=0, keepdims=True)
        v_t = vt_ref[n, h * dh:(h + 1) * dh, :]
        m_old, l_old, acc = state
        m_new = jnp.maximum(m_old, mx)
        alpha = jnp.exp2(m_old - m_new)
        pr = jnp.exp2(s - m_new)
        return (m_new, alpha * l_old + jnp.sum(pr, axis=0, keepdims=True),
                alpha * acc + _dot(v_t, pr.astype(BF16)))

    def run_tiles(jobs, states):
        states = list(states)
        pending = {}
        for i in range(min(MOBA_LOOKAHEAD, len(jobs))):
            pending[i] = scores(*jobs[i][:2])
        for i, (h, n, tile) in enumerate(jobs):
            s = pending.pop(i)
            if tile is not None:
                s = tile + s
            states[h] = absorb(h, n, s, states[h])
            if i + MOBA_LOOKAHEAD < len(jobs):
                pending[i + MOBA_LOOKAHEAD] = scores(*jobs[i + MOBA_LOOKAHEAD][:2])
        return tuple(states)

    states = tuple((jnp.full((1, bs), NEG_INF, F32), jnp.zeros((1, bs), F32), jnp.zeros((dh, bs), F32))
                   for _ in range(nh))

    def near_body(k, carry):
        return run_tiles([(h, qi - k, tiles_ref[h, k]) for h in range(nh)], carry)

    def far_body(n, carry):
        return run_tiles([(h, n, None) for h in range(nh)], carry)

    states = lax.fori_loop(0, jnp.minimum(qi + 1, 2), near_body, states)
    states = lax.fori_loop(0, jnp.maximum(qi - 1, 0), far_body, states)
    out_t = jnp.concatenate([acc * (1.0 / l) for (_, l, acc) in states], axis=0)
    out_ref[...] = out_t.T


def _moba(aq, ak, av, tiles, far, bsz, seq):
    t = bsz * seq
    bs = MOBA_BLOCK
    nb = seq // bs
    return pl.pallas_call(
        functools.partial(_moba_kernel, nb=nb),
        grid=(bsz, nb),
        in_specs=[pl.BlockSpec((bs, MOBA_WIDTH), lambda b, i: (b * nb + i, 0)),
                  pl.BlockSpec((seq, MOBA_WIDTH), lambda b, i: (b, 0)),
                  pl.BlockSpec((seq, MOBA_WIDTH), lambda b, i: (b, 0)),
                  pl.BlockSpec(tiles.shape, lambda b, i: (0, 0, 0, 0)),
                  pl.BlockSpec(far.shape, lambda b, i: (0, 0))],
        out_specs=pl.BlockSpec((bs, MOBA_WIDTH), lambda b, i: (b * nb + i, 0)),
        out_shape=jax.ShapeDtypeStruct((t, MOBA_WIDTH), F32),
        scratch_shapes=[pltpu.VMEM((MOBA_HEADS, nb, bs, LANES), BF16),
                        pltpu.VMEM((nb, MOBA_WIDTH, bs), BF16),
                        pltpu.VMEM((LANES, MOBA_WIDTH), F32),
                        pltpu.VMEM((MOBA_HEADS, LANES, bs), BF16)],
        compiler_params=pltpu.CompilerParams(dimension_semantics=("parallel", "arbitrary"),
                                             vmem_limit_bytes=VMEM_LIMIT),
        name="moba",
    )(aq, ak, av, tiles, far)


def _merge_kernel(x_ref, ys_ref, ym_ref, ya_ref, gate_ref, wp_ref, wo_ref, g_ref, o_ref):
    def gate(i):
        return gate_ref[:, i * D_MODEL:(i + 1) * D_MODEL].astype(F32)

    merged = (gate(0) * _dot(ys_ref[...].astype(BF16), wp_ref[0])
              + gate(1) * _dot(ym_ref[...].astype(BF16), wp_ref[1])
              + gate(2) * _dot(ya_ref[...].astype(BF16), wp_ref[2]))
    z = _dot(merged.astype(BF16), wo_ref[...])
    o_ref[...] = x_ref[...] + _rms(z, g_ref[...])


def _merge(x2d, ys, ym, ya, gate, wp, wo, g, ts):
    t = x2d.shape[0]
    row = lambda i: (i, 0)
    full = lambda i: (0, 0)
    return pl.pallas_call(
        _merge_kernel,
        grid=(t // ts,),
        in_specs=[pl.BlockSpec((ts, D_MODEL), row),
                  pl.BlockSpec((ts, SSM_WIDTH), row),
                  pl.BlockSpec((ts, MLSTM_WIDTH), row),
                  pl.BlockSpec((ts, MOBA_WIDTH), row),
                  pl.BlockSpec((ts, 3 * D_MODEL), row),
                  pl.BlockSpec(wp.shape, lambda i: (0, 0, 0)),
                  pl.BlockSpec(wo.shape, full),
                  pl.BlockSpec((1, D_MODEL), full)],
        out_specs=pl.BlockSpec((ts, D_MODEL), row),
        out_shape=jax.ShapeDtypeStruct((t, D_MODEL), F32),
        compiler_params=pltpu.CompilerParams(dimension_semantics=("parallel",),
                                             vmem_limit_bytes=VMEM_LIMIT),
        name="merge",
    )(x2d, ys, ym, ya, gate, wp, wo, g)


def _ffn_kernel(x_ref, g1_ref, w1_ref, w2_ref, g2_ref, o_ref):
    x = x_ref[...]
    hb = _rms(x, g1_ref[...]).astype(BF16)
    a = jnp.maximum(_dot(hb, w1_ref[...]), 0.0)
    f = _dot((a * a).astype(BF16), w2_ref[...])
    o_ref[...] = x + _rms(f, g2_ref[...])


def _ffn(x2d, g1, w1, w2, g2, ts):
    t = x2d.shape[0]
    full = lambda i: (0, 0)
    return pl.pallas_call(
        _ffn_kernel,
        grid=(t // ts,),
        in_specs=[pl.BlockSpec((ts, D_MODEL), lambda i: (i, 0)),
                  pl.BlockSpec((1, D_MODEL), full),
                  pl.BlockSpec((D_MODEL, D_FF), full, pipeline_mode=pl.Buffered(1)),
                  pl.BlockSpec((D_FF, D_MODEL), full, pipeline_mode=pl.Buffered(1)),
                  pl.BlockSpec((1, D_MODEL), full)],
        out_specs=pl.BlockSpec((ts, D_MODEL), lambda i: (i, 0)),
        out_shape=jax.ShapeDtypeStruct((t, D_MODEL), F32),
        compiler_params=pltpu.CompilerParams(dimension_semantics=("parallel",),
                                             vmem_limit_bytes=VMEM_LIMIT),
        name="ffn",
    )(x2d, g1, w1, w2, g2)


def _reorder_w_in(w_in):
    w = w_in.astype(BF16)
    n_if = 2 * MLSTM_HEADS
    c_if = SSM_WIDTH + 4 * MLSTM_WIDTH
    gif = jnp.pad(w[..., c_if:c_if + n_if], ((0, 0), (0, 0), (0, LANES - n_if)))
    return jnp.concatenate([w[..., :c_if], w[..., c_if + n_if:], gif], axis=-1)


def kernel(x, w_in, conv_w, ssm_a_re, ssm_a_im, ssm_log_dt, ssm_b_re, ssm_b_im, ssm_c_re, ssm_c_im, ssm_d, ssm_w_glu, mlstm_i_bias, mlstm_f_bias, mlstm_head_gain, rel_bias, w_ssm_proj, w_mlstm_proj, w_moba_proj, w_out, w_ff1, w_ff2, norm_mix_pre, norm_mix_post, norm_ffn_pre, norm_ffn_post):
    bsz, seq, _ = x.shape
    depth = w_in.shape[0]
    assert bsz == SUBLANES and seq % MOBA_BLOCK == 0 and seq // MOBA_BLOCK <= SUBLANES
    ts = 256
    tc = 64
    x2d = x.reshape(bsz * seq, D_MODEL)
    w_in_r = _reorder_w_in(w_in)
    wp = jnp.stack([w_ssm_proj, w_mlstm_proj, w_moba_proj], axis=1).astype(BF16)
    wo = w_out.astype(BF16)
    w1 = w_ff1.astype(BF16)
    w2 = w_ff2.astype(BF16)
    wglu = ssm_w_glu.astype(BF16)
    gate_bias = jnp.pad(jnp.concatenate([mlstm_i_bias, mlstm_f_bias], axis=-1).astype(F32),
                        ((0, 0), (0, LANES - 2 * MLSTM_HEADS)))
    tiles, far = _moba_bias_tiles(rel_bias)
    for l in range(depth):
        u, qk, mv, mo, aq, ak, av, gate, gif = _inproj(x2d, norm_mix_pre[l][None, :], w_in_r[l], ts)
        bblk, cblk, ar_b, ai_b = _ssm_params(ssm_a_re[l], ssm_a_im[l], ssm_log_dt[l], ssm_b_re[l], ssm_b_im[l],
                                             ssm_c_re[l], ssm_c_im[l], bsz)
        y_ssm = _ssm(u.reshape(bsz, seq, SSM_WIDTH), bblk, cblk, ar_b, ai_b, ssm_d[l][None, :], wglu[l],
                     bsz, seq, tc)
        y_mlstm = _mlstm(qk, mv, mo, gif, conv_w[l].astype(F32), gate_bias[l][None, :],
                         mlstm_head_gain[l][None, :].astype(F32), bsz, seq)
        y_moba = _moba(aq, ak, av, tiles, far, bsz, seq)
        x2d = _merge(x2d, y_ssm.reshape(bsz * seq, SSM_WIDTH), y_mlstm, y_moba, gate, wp[l], wo[l],
                     norm_mix_post[l][None, :], ts)
        x2d = _ffn(x2d, norm_ffn_pre[l][None, :], w1[l], w2[l], norm_ffn_post[l][None, :], ts)
    return x2d.reshape(bsz, seq, D_MODEL)
```

```python
import functools
import math

import jax
import jax.numpy as jnp
from jax import lax
from jax.experimental import pallas as pl
from jax.experimental.pallas import tpu as pltpu

F32 = jnp.float32
BF16 = jnp.bfloat16

D_MODEL = 1024
SSM_WIDTH = 512
SSM_GROUP = 16
SSM_GROUPS = 32
SSM_STATE = 64
MLSTM_WIDTH = 512
MLSTM_HEADS = 4
MLSTM_HEAD_DIM = 128
MLSTM_CHUNK = 128
CONV_WIDTH = 4
MOBA_WIDTH = 512
MOBA_HEADS = 8
MOBA_HEAD_DIM = 64
MOBA_BLOCK = 256
MOBA_TOPK = 3
REL_BUCKETS = 32
REL_MAX_DIST = 128
D_FF = 4 * D_MODEL
RMS_EPS = 1e-6
NEG_INF = -1e30

LANES = 128
SUBLANES = 8
VMEM_LIMIT = 56 * 1024 * 1024

C_U = 0
C_QK = 512
C_MV = 1536
C_MO = 2048
C_AQ = 2560
C_AK = 3072
C_AV = 3584
C_GATE = 4096
C_GIF = 7168
C_END = 7296


def _rms(x, g):
    return x * lax.rsqrt(jnp.mean(x * x, axis=-1, keepdims=True) + RMS_EPS) * g


def _sigmoid(x):
    return 1.0 / (1.0 + jnp.exp(-x))


def _dot(a, b):
    return jnp.dot(a, b, preferred_element_type=F32)


def _dot_nt(a, b):
    return lax.dot_general(a, b, (((1,), (1,)), ((), ())), preferred_element_type=F32)


def _split3(a):
    hi = a.astype(BF16)
    r1 = a - hi.astype(F32)
    mid = r1.astype(BF16)
    lo = (r1 - mid.astype(F32)).astype(BF16)
    return hi, mid, lo


def _inproj_kernel(x_ref, g_ref, wa_ref, wb_ref, wg_ref, u_ref, qk_ref, mv_ref, mo_ref, aq_ref, ak_ref, av_ref,
                   gate_ref, gif_ref):
    hb = _rms(x_ref[...], g_ref[...]).astype(BF16)

    def proj(w_ref, a, b):
        return _dot(hb, w_ref[:, a:b])

    u_ref[...] = proj(wa_ref, C_U, C_QK)
    qk_ref[...] = proj(wa_ref, C_QK, C_MV)
    mv_ref[...] = proj(wa_ref, C_MV, C_MO)
    mo_ref[...] = proj(wa_ref, C_MO, C_AQ)
    aq_ref[...] = proj(wb_ref, C_AQ - C_AQ, C_AK - C_AQ)
    ak_ref[...] = proj(wb_ref, C_AK - C_AQ, C_AV - C_AQ)
    av_ref[...] = proj(wb_ref, C_AV - C_AQ, C_GATE - C_AQ)
    gate_ref[...] = _sigmoid(proj(wb_ref, C_GATE - C_AQ, C_GIF - C_AQ)).astype(BF16)
    gif_ref[...] = _dot(hb, wg_ref[...])


def _inproj(x2d, g, wa, wb, wg, ts):
    t = x2d.shape[0]
    row = lambda i: (i, 0)
    full = lambda i: (0, 0)

    def out(n, dtype=F32):
        return jax.ShapeDtypeStruct((t, n), dtype), pl.BlockSpec((ts, n), row)

    outs = [out(512), out(1024), out(512), out(512), out(512), out(512), out(512), out(3 * D_MODEL, BF16),
            out(LANES)]
    return pl.pallas_call(
        _inproj_kernel,
        grid=(t // ts,),
        in_specs=[pl.BlockSpec((ts, D_MODEL), row),
                  pl.BlockSpec((1, D_MODEL), full),
                  pl.BlockSpec(wa.shape, full, pipeline_mode=pl.Buffered(1)),
                  pl.BlockSpec(wb.shape, full, pipeline_mode=pl.Buffered(1)),
                  pl.BlockSpec(wg.shape, full, pipeline_mode=pl.Buffered(1))],
        out_specs=[o[1] for o in outs],
        out_shape=[o[0] for o in outs],
        compiler_params=pltpu.CompilerParams(dimension_semantics=("parallel",),
                                             vmem_limit_bytes=VMEM_LIMIT),
        name="inproj",
    )(x2d, g, wa, wb, wg)


SSM_KB = 2
SSM_KB_CH = SSM_WIDTH // SSM_KB
SSM_KB_ST = SSM_KB_CH // SSM_GROUP * SSM_STATE
SSM_SLAB = 512


def _gelu(x):
    return 0.5 * x * (1.0 + jnp.tanh(math.sqrt(2.0 / math.pi) * (x + 0.044715 * (x * x * x))))


def _ssm_kernel(u_ref, perm_ref, perm_t_ref, bblk_ref, cblk_ref, ar_ref, ai_ref, d_ref, wglu_ref, o_ref,
                xs_ref, st_ref, *, bsz, tc):
    @pl.when(pl.program_id(0) == 0)
    def _():
        st_ref[...] = jnp.zeros_like(st_ref)

    u_bm = u_ref[...].reshape(bsz * tc, SSM_WIDTH)
    u_hi = u_bm.astype(BF16)
    u_lo = (u_bm - u_hi.astype(F32)).astype(BF16)
    u_tm_hi = _dot(perm_ref[...], u_hi)
    u = u_tm_hi + _dot(perm_ref[...], u_lo)
    ub = u_tm_hi.astype(BF16)
    for kb in range(SSM_KB):
        xs_ref[:, kb * 2 * SSM_KB_ST:(kb + 1) * 2 * SSM_KB_ST] = _dot(
            ub[:, kb * SSM_KB_CH:(kb + 1) * SSM_KB_CH], bblk_ref[kb])

    for kb in range(SSM_KB):
        for j in range(SSM_KB_ST // SSM_SLAB):
            re0 = kb * 2 * SSM_KB_ST + j * SSM_SLAB
            im0 = re0 + SSM_KB_ST
            co = kb * SSM_KB_ST + j * SSM_SLAB
            ar = ar_ref[:, co:co + SSM_SLAB]
            ai = ai_ref[:, co:co + SSM_SLAB]

            def body(t, carry, re0=re0, im0=im0, ar=ar, ai=ai):
                xr, xi = carry
                r0 = pl.multiple_of(t * bsz, bsz)
                br = xs_ref[pl.ds(r0, bsz), re0:re0 + SSM_SLAB]
                bi = xs_ref[pl.ds(r0, bsz), im0:im0 + SSM_SLAB]
                nr = ar * xr - ai * xi + br
                ni = ar * xi + ai * xr + bi
                xs_ref[pl.ds(r0, bsz), re0:re0 + SSM_SLAB] = nr
                xs_ref[pl.ds(r0, bsz), im0:im0 + SSM_SLAB] = ni
                return nr, ni

            xr, xi = lax.fori_loop(0, tc, body,
                                   (st_ref[:, re0:re0 + SSM_SLAB], st_ref[:, im0:im0 + SSM_SLAB]),
                                   unroll=8)
            st_ref[:, re0:re0 + SSM_SLAB] = xr
            st_ref[:, im0:im0 + SSM_SLAB] = xi

    ys = []
    for kb in range(SSM_KB):
        xb = xs_ref[:, kb * 2 * SSM_KB_ST:(kb + 1) * 2 * SSM_KB_ST].astype(BF16)
        ys.append(_dot(xb, cblk_ref[kb]))
    y = jnp.concatenate(ys, axis=-1) + d_ref[...] * u
    yg = _gelu(y)
    out_tm = (yg * _sigmoid(_dot(yg.astype(BF16), wglu_ref[...]))).astype(BF16)
    o_ref[...] = _dot(perm_t_ref[...], out_tm).astype(BF16).reshape(bsz, tc, SSM_WIDTH)


def _ssm(u3, bblk, cblk, ar, ai, d, wglu, bsz, seq, tc):
    rows = tc * bsz
    full2 = lambda c: (0, 0)
    full3 = lambda c: (0, 0, 0)
    r = jnp.arange(rows)
    perm = (r[None, :] == (r[:, None] % bsz) * tc + r[:, None] // bsz).astype(BF16)
    return pl.pallas_call(
        functools.partial(_ssm_kernel, bsz=bsz, tc=tc),
        grid=(seq // tc,),
        in_specs=[pl.BlockSpec((bsz, tc, SSM_WIDTH), lambda c: (0, c, 0)),
                  pl.BlockSpec((rows, rows), full2),
                  pl.BlockSpec((rows, rows), full2),
                  pl.BlockSpec(bblk.shape, full3),
                  pl.BlockSpec(cblk.shape, full3),
                  pl.BlockSpec(ar.shape, full2),
                  pl.BlockSpec(ai.shape, full2),
                  pl.BlockSpec((1, SSM_WIDTH), full2),
                  pl.BlockSpec((SSM_WIDTH, SSM_WIDTH), full2)],
        out_specs=pl.BlockSpec((bsz, tc, SSM_WIDTH), lambda c: (0, c, 0)),
        out_shape=jax.ShapeDtypeStruct((bsz, seq, SSM_WIDTH), BF16),
        scratch_shapes=[pltpu.VMEM((rows, SSM_KB * 2 * SSM_KB_ST), F32),
                        pltpu.VMEM((bsz, SSM_KB * 2 * SSM_KB_ST), F32)],
        compiler_params=pltpu.CompilerParams(dimension_semantics=("arbitrary",),
                                             vmem_limit_bytes=VMEM_LIMIT),
        name="ssm",
    )(u3, perm, perm.T, bblk, cblk, ar, ai, d, wglu)


def _ssm_params(a_re, a_im, log_dt, b_re, b_im, c_re, c_im, bsz):
    ar, ai = a_re.astype(F32), a_im.astype(F32)
    dt = jnp.exp(log_dt.astype(F32))[:, None]
    decay = jnp.exp(dt * ar)
    abar_r, abar_i = decay * jnp.cos(dt * ai), decay * jnp.sin(dt * ai)
    den = ar * ar + ai * ai
    nr, ni = abar_r - 1.0, abar_i
    fr, fi = (nr * ar + ni * ai) / den, (ni * ar - nr * ai) / den
    br, bi = b_re.astype(F32), b_im.astype(F32)
    bbar_r = fr[..., None] * br - fi[..., None] * bi
    bbar_i = fr[..., None] * bi + fi[..., None] * br
    gl = SSM_GROUPS // SSM_KB
    eye = jnp.eye(gl, dtype=F32)

    def pack_b(bb):
        bb = bb.reshape(SSM_KB, gl, SSM_STATE, SSM_GROUP)
        dense = jnp.einsum('kgph,gq->kghqp', bb, eye)
        return dense.reshape(SSM_KB, gl * SSM_GROUP, gl * SSM_STATE)

    def pack_c(cc):
        cc = cc.reshape(SSM_KB, gl, SSM_GROUP, SSM_STATE)
        dense = jnp.einsum('kghp,gq->kgpqh', cc, eye)
        return dense.reshape(SSM_KB, gl * SSM_STATE, gl * SSM_GROUP)

    bblk = jnp.concatenate([pack_b(bbar_r), pack_b(bbar_i)], axis=-1).astype(BF16)
    cblk = jnp.concatenate([pack_c(c_re.astype(F32)), pack_c(-c_im.astype(F32))], axis=1).astype(BF16)
    ar_b = jnp.broadcast_to(abar_r.reshape(1, -1), (bsz, SSM_GROUPS * SSM_STATE))
    ai_b = jnp.broadcast_to(abar_i.reshape(1, -1), (bsz, SSM_GROUPS * SSM_STATE))
    return bblk, cblk, ar_b, ai_b


def _mlstm_kernel(qk_ref, v_ref, o_ref, gif_ref, convw_ref, bias_ref, gain_ref, out_ref,
                  xp_ref, ct_ref, n_ref, m_ref):
    lc, dh, nh = MLSTM_CHUNK, MLSTM_HEAD_DIM, MLSTM_HEADS
    halo = SUBLANES

    @pl.when(pl.program_id(1) == 0)
    def _():
        xp_ref[0:halo, :] = jnp.zeros((halo, 2 * MLSTM_WIDTH), F32)
        ct_ref[...] = jnp.zeros_like(ct_ref)
        n_ref[...] = jnp.zeros_like(n_ref)
        m_ref[...] = jnp.zeros_like(m_ref)

    xp_ref[halo:halo + lc, :] = qk_ref[...]
    conv = convw_ref[CONV_WIDTH - 1:CONV_WIDTH, :] * xp_ref[halo:halo + lc, :]
    for j in range(CONV_WIDTH - 1):
        off = halo - (CONV_WIDTH - 1) + j
        conv = conv + convw_ref[j:j + 1, :] * xp_ref[off:off + lc, :]
    xp_ref[0:halo, :] = xp_ref[lc:lc + halo, :]
    qk = conv * _sigmoid(conv)

    gpre = gif_ref[...] + bias_ref[...]
    lsig = jnp.minimum(gpre, 0.0) - jnp.log(1.0 + jnp.exp(-jnp.abs(gpre)))
    row_i = lax.broadcasted_iota(jnp.int32, (lc, lc), 0)
    col_i = lax.broadcasted_iota(jnp.int32, (lc, lc), 1)
    causal = col_i <= row_i
    tri = jnp.where(causal, 1.0, 0.0).astype(BF16)
    tri_t = jnp.where(row_i <= col_i, 1.0, 0.0).astype(BF16)
    l_hi, l_mid, l_lo = _split3(lsig)
    bcum_c = _dot(tri, l_hi) + _dot(tri, l_mid) + _dot(tri, l_lo)
    lsig_t = lsig.T
    t_hi, t_mid, t_lo = _split3(lsig_t)
    bcum_r = _dot(t_hi, tri_t) + _dot(t_mid, tri_t) + _dot(t_lo, tri_t)
    gpre_t = gpre.T
    per_time = jnp.where(col_i < nh, gpre, bcum_c)
    rep_rows = lax.broadcasted_iota(jnp.int32, (LANES, 2 * nh * LANES), 0)
    rep_cols = lax.broadcasted_iota(jnp.int32, (LANES, 2 * nh * LANES), 1)
    spread = jnp.where(rep_rows == rep_cols // LANES, 1.0, 0.0).astype(BF16)
    p_hi, p_mid, p_lo = _split3(per_time)
    rep = _dot(p_hi, spread) + _dot(p_mid, spread) + _dot(p_lo, spread)

    hs = range(nh)
    qf = [qk[:, h * dh:(h + 1) * dh] for h in hs]
    q = [x.astype(BF16) for x in qf]
    k = [qk[:, MLSTM_WIDTH + h * dh:MLSTM_WIDTH + (h + 1) * dh] * (dh ** -0.5) for h in hs]
    kb = [x.astype(BF16) for x in k]
    v = [v_ref[:, h * dh:(h + 1) * dh] for h in hs]
    bc = [rep[:, (nh + h) * LANES:(nh + h + 1) * LANES] for h in hs]
    brow = [bcum_r[nh + h:nh + h + 1, :] for h in hs]
    ic = [rep[:, h * LANES:(h + 1) * LANES] for h in hs]
    irow = [gpre_t[h:h + 1, :] for h in hs]
    gtot = [x[lc - 1:lc, :] for x in bc]
    m_prev = [m_ref[h:h + 1, :] for h in hs]
    n_prev = [n_ref[h:h + 1, :] for h in hs]
    ct_prev = [ct_ref[h] for h in hs]

    qk_t = [_dot_nt(q[h], kb[h]) for h in hs]
    q_ct = [_dot(q[h], ct_prev[h].astype(BF16)) for h in hs]
    m_loc = [jnp.max(gtot[h] - brow[h] + irow[h], axis=-1, keepdims=True) for h in hs]
    wgt_c = [jnp.exp(gtot[h] - bc[h] + ic[h] - m_loc[h]) for h in hs]
    d_ct = [_dot(k[h].T.astype(BF16), (wgt_c[h] * v[h]).astype(BF16)) for h in hs]

    log_d = [jnp.where(causal, bc[h] - brow[h] + irow[h], NEG_INF) for h in hs]
    m_intra = [jnp.max(log_d[h], axis=-1, keepdims=True) for h in hs]
    qn = [jnp.sum(qf[h] * n_prev[h], axis=-1, keepdims=True) for h in hs]
    log_inter = [bc[h] + m_prev[h] for h in hs]
    m_q = [jnp.maximum(log_inter[h], m_intra[h]) for h in hs]
    s_mat = [qk_t[h] * jnp.exp(log_d[h] - m_q[h]) for h in hs]
    s_sum = [jnp.sum(s_mat[h], axis=-1, keepdims=True) for h in hs]
    s_v = [_dot(s_mat[h].astype(BF16), v[h].astype(BF16)) for h in hs]
    inter = [jnp.exp(log_inter[h] - m_q[h]) for h in hs]
    hh = [(inter[h] * q_ct[h] + s_v[h])
          / jnp.maximum(jnp.abs(s_sum[h] + inter[h] * qn[h]), jnp.exp(-m_q[h])) for h in hs]
    ms = [jnp.mean(hh[h] * hh[h], axis=-1, keepdims=True) for h in hs]
    for h in hs:
        og = _sigmoid(o_ref[:, h * dh:(h + 1) * dh])
        out_ref[:, h * dh:(h + 1) * dh] = og * (hh[h] * lax.rsqrt(ms[h] + RMS_EPS)
                                                * gain_ref[:, h * dh:(h + 1) * dh])

    for h in hs:
        d_n = jnp.sum(wgt_c[h] * k[h], axis=0, keepdims=True)
        m_new = jnp.maximum(gtot[h] + m_prev[h], m_loc[h])
        a = jnp.exp(gtot[h] + m_prev[h] - m_new)
        bb = jnp.exp(m_loc[h] - m_new)
        ct_ref[h] = a * ct_prev[h] + bb * d_ct[h]
        n_ref[h:h + 1, :] = a * n_prev[h] + bb * d_n
        m_ref[h:h + 1, :] = m_new


def _mlstm(qk, mv, mo, gif, convw, bias, gain, bsz, seq):
    t = bsz * seq
    lc = MLSTM_CHUNK
    nc = seq // lc
    row = lambda b, c: (b * nc + c, 0)
    full = lambda b, c: (0, 0)
    return pl.pallas_call(
        _mlstm_kernel,
        grid=(bsz, nc),
        in_specs=[pl.BlockSpec((lc, 2 * MLSTM_WIDTH), row),
                  pl.BlockSpec((lc, MLSTM_WIDTH), row),
                  pl.BlockSpec((lc, MLSTM_WIDTH), row),
                  pl.BlockSpec((lc, LANES), row),
                  pl.BlockSpec((CONV_WIDTH, 2 * MLSTM_WIDTH), full),
                  pl.BlockSpec((1, LANES), full),
                  pl.BlockSpec((1, MLSTM_WIDTH), full)],
        out_specs=pl.BlockSpec((lc, MLSTM_WIDTH), row),
        out_shape=jax.ShapeDtypeStruct((t, MLSTM_WIDTH), F32),
        scratch_shapes=[pltpu.VMEM((lc + SUBLANES, 2 * MLSTM_WIDTH), F32),
                        pltpu.VMEM((MLSTM_HEADS, MLSTM_HEAD_DIM, MLSTM_HEAD_DIM), F32),
                        pltpu.VMEM((SUBLANES, MLSTM_HEAD_DIM), F32),
                        pltpu.VMEM((SUBLANES, LANES), F32)],
        compiler_params=pltpu.CompilerParams(dimension_semantics=("parallel", "arbitrary"),
                                             vmem_limit_bytes=VMEM_LIMIT),
        name="mlstm",
    )(qk, mv, mo, gif, convw, bias, gain)


def _rel_bucket(dist):
    max_exact = REL_BUCKETS // 2
    is_small = dist < max_exact
    large = max_exact + (jnp.log(jnp.maximum(dist, 1).astype(F32) / max_exact)
                         / math.log(REL_MAX_DIST / max_exact) * (REL_BUCKETS - max_exact)).astype(jnp.int32)
    large = jnp.minimum(large, REL_BUCKETS - 1)
    return jnp.where(is_small, dist, large)


LOG2E = math.log2(math.e)
MOBA_LOOKAHEAD = 8


def _moba_bias_tiles(rel_bias):
    bs = MOBA_BLOCK
    d_own = jnp.arange(bs)[None, :] - jnp.arange(bs)[:, None]
    tab = rel_bias.astype(F32) * LOG2E
    buckets = jnp.stack([_rel_bucket(jnp.maximum(d_own, 0)), _rel_bucket(bs + d_own)])
    onehot = (buckets[..., None] == jnp.arange(REL_BUCKETS)).astype(F32)
    tiles = jnp.einsum('tjib,hb->htji', onehot, tab, precision=lax.Precision.HIGHEST)
    causal = jnp.stack([d_own >= 0, jnp.ones_like(d_own, dtype=bool)])
    tiles = jnp.where(causal[None], tiles, NEG_INF)
    far = jnp.take(tab, _rel_bucket(jnp.full((1,), 2 * bs, jnp.int32)), axis=1)
    far_rows = jnp.broadcast_to(far[:, None, :], (MOBA_HEADS, SUBLANES, LANES)).reshape(-1, LANES)
    return tiles, jnp.pad(far_rows, ((0, LANES - MOBA_HEADS * SUBLANES), (0, 0)))


def _moba_kernel(q_ref, k_ref, v_ref, tiles_ref, far_ref, out_ref,
                 kaug_ref, vt_ref, km_ref, qaug_ref, *, nb):
    bs, nh, dh = MOBA_BLOCK, MOBA_HEADS, MOBA_HEAD_DIM
    qi = pl.program_id(1)

    @pl.when(qi == 0)
    def _():
        lane = lax.broadcasted_iota(jnp.int32, (bs, LANES), 1)
        means = []
        for n in range(nb):
            kblk = k_ref[n * bs:(n + 1) * bs, :]
            means.append(jnp.mean(kblk, axis=0, keepdims=True))
            for h in range(nh):
                p, e = divmod(h, 2)
                o = (1 - e) * dh
                hot = jnp.where(lane == o + n, 1.0, jnp.where(lane == o + SUBLANES + n, 1.0, 0.0))
                kaug_ref[h, n] = jnp.where(lane // dh == e, kblk[:, p * LANES:(p + 1) * LANES], hot).astype(BF16)
            vt_ref[n] = v_ref[n * bs:(n + 1) * bs, :].T.astype(BF16)
        means += [jnp.zeros_like(means[0])] * (SUBLANES - nb)
        kmean = jnp.concatenate(means, axis=0)
        rows = lax.broadcasted_iota(jnp.int32, (LANES, MOBA_WIDTH), 0)
        cols = lax.broadcasted_iota(jnp.int32, (LANES, MOBA_WIDTH), 1)
        tiled = jnp.concatenate([kmean] * (LANES // SUBLANES), axis=0)
        km_ref[...] = jnp.where(rows // SUBLANES == cols // dh, tiled, 0.0)

    q = q_ref[...]
    q_hi, q_mid, _ = _split3(q)
    m_hi, m_mid, _ = _split3(km_ref[...])
    gate = _dot_nt(m_hi, q_hi) + _dot_nt(m_mid, q_hi) + _dot_nt(m_hi, q_mid)
    nrows = LANES
    blk = lax.broadcasted_iota(jnp.int32, (nrows, bs), 0) % SUBLANES
    gate = jnp.where(blk < qi, gate, NEG_INF)
    rank = jnp.zeros((nrows, bs), F32)
    for s in range(1, SUBLANES):
        fwd = pltpu.roll(gate, nrows - s, 0)
        back = pltpu.roll(gate, SUBLANES - s, 0)
        wrapped = blk + s >= SUBLANES
        other = jnp.where(wrapped, back, fwd)
        rank = rank + jnp.where(wrapped, jnp.where(other >= gate, 1.0, 0.0), jnp.where(other > gate, 1.0, 0.0))
    chosen = rank < MOBA_TOPK
    far = jnp.concatenate([far_ref[...]] * (bs // LANES), axis=1)
    far_hi = far.astype(BF16).astype(F32)
    far_lo = far - far_hi
    sel_hi = jnp.where(blk < qi - 1, jnp.where(chosen, far_hi, NEG_INF),
                       jnp.where(blk == qi - 1, jnp.where(chosen, 0.0, NEG_INF),
                                 jnp.where(blk == qi, 0.0, NEG_INF)))
    sel_lo = jnp.where(blk < qi - 1, jnp.where(chosen, far_lo, 0.0), 0.0)

    q_t = (q * (dh ** -0.5 * LOG2E)).T
    pad = jnp.zeros((dh - 2 * SUBLANES, bs), F32)
    for h in range(nh):
        e = h % 2
        extra = [sel_hi[h * SUBLANES:(h + 1) * SUBLANES], sel_lo[h * SUBLANES:(h + 1) * SUBLANES], pad]
        q_h = [q_t[h * dh:(h + 1) * dh]]
        qaug_ref[h] = jnp.concatenate(q_h + extra if e == 0 else extra + q_h, axis=0).astype(BF16)

    def scores(h, n):
        return _dot(kaug_ref[h, n], qaug_ref[h])

    def absorb(h, n, s, state):
        mx = jnp.max(s, axis=0, keepdims=True)
        v_t = vt_ref[n, h * dh:(h + 1) * dh, :]
        m_old, l_old, acc = state
        m_new = jnp.maximum(m_old, mx)
        alpha = jnp.exp2(m_old - m_new)
        pr = jnp.exp2(s - m_new)
        return (m_new, alpha * l_old + jnp.sum(pr, axis=0, keepdims=True),
                alpha * acc + _dot(v_t, pr.astype(BF16)))

    def run_tiles(jobs, states):
        states = list(states)
        pending = {}
        for i in range(min(MOBA_LOOKAHEAD, len(jobs))):
            pending[i] = scores(*jobs[i][:2])
        for i, (h, n, tile) in enumerate(jobs):
            s = pending.pop(i)
            if tile is not None:
                s = tile + s
            states[h] = absorb(h, n, s, states[h])
            if i + MOBA_LOOKAHEAD < len(jobs):
                pending[i + MOBA_LOOKAHEAD] = scores(*jobs[i + MOBA_LOOKAHEAD][:2])
        return tuple(states)

    states = tuple((jnp.full((1, bs), NEG_INF, F32), jnp.zeros((1, bs), F32), jnp.zeros((dh, bs), F32))
                   for _ in range(nh))

    def near_body(k, carry):
        return run_tiles([(h, qi - k, tiles_ref[h, k]) for h in range(nh)], carry)

    def far_body(n, carry):
        return run_tiles([(h, n, None) for h in range(nh)], carry)

    states = lax.fori_loop(0, jnp.minimum(qi + 1, 2), near_body, states)
    states = lax.fori_loop(0, jnp.maximum(qi - 1, 0), far_body, states)
    out_t = jnp.concatenate([acc * (1.0 / l) for (_, l, acc) in states], axis=0)
    out_ref[...] = out_t.T


def _moba(aq, ak, av, tiles, far, bsz, seq):
    t = bsz * seq
    bs = MOBA_BLOCK
    nb = seq // bs
    return pl.pallas_call(
        functools.partial(_moba_kernel, nb=nb),
        grid=(bsz, nb),
        in_specs=[pl.BlockSpec((bs, MOBA_WIDTH), lambda b, i: (b * nb + i, 0)),
                  pl.BlockSpec((seq, MOBA_WIDTH), lambda b, i: (b, 0)),
                  pl.BlockSpec((seq, MOBA_WIDTH), lambda b, i: (b, 0)),
                  pl.BlockSpec(tiles.shape, lambda b, i: (0, 0, 0, 0)),
                  pl.BlockSpec(far.shape, lambda b, i: (0, 0))],
        out_specs=pl.BlockSpec((bs, MOBA_WIDTH), lambda b, i: (b * nb + i, 0)),
        out_shape=jax.ShapeDtypeStruct((t, MOBA_WIDTH), F32),
        scratch_shapes=[pltpu.VMEM((MOBA_HEADS, nb, bs, LANES), BF16),
                        pltpu.VMEM((nb, MOBA_WIDTH, bs), BF16),
                        pltpu.VMEM((LANES, MOBA_WIDTH), F32),
                        pltpu.VMEM((MOBA_HEADS, LANES, bs), BF16)],
        compiler_params=pltpu.CompilerParams(dimension_semantics=("parallel", "arbitrary"),
                                             vmem_limit_bytes=VMEM_LIMIT),
        name="moba",
    )(aq, ak, av, tiles, far)


def _merge_kernel(x_ref, ys_ref, ym_ref, ya_ref, gate_ref, wps_ref, wpm_ref, wpa_ref, wo_ref, g_ref, o_ref):
    def gate(i):
        return gate_ref[:, i * D_MODEL:(i + 1) * D_MODEL].astype(F32)

    merged = (gate(0) * _dot(ys_ref[...].astype(BF16), wps_ref[...])
              + gate(1) * _dot(ym_ref[...].astype(BF16), wpm_ref[...])
              + gate(2) * _dot(ya_ref[...].astype(BF16), wpa_ref[...]))
    z = _dot(merged.astype(BF16), wo_ref[...])
    o_ref[...] = x_ref[...] + _rms(z, g_ref[...])


def _merge(x2d, ys, ym, ya, gate, wps, wpm, wpa, wo, g, ts):
    t = x2d.shape[0]
    row = lambda i: (i, 0)
    full = lambda i: (0, 0)
    return pl.pallas_call(
        _merge_kernel,
        grid=(t // ts,),
        in_specs=[pl.BlockSpec((ts, D_MODEL), row),
                  pl.BlockSpec((ts, SSM_WIDTH), row),
                  pl.BlockSpec((ts, MLSTM_WIDTH), row),
                  pl.BlockSpec((ts, MOBA_WIDTH), row),
                  pl.BlockSpec((ts, 3 * D_MODEL), row),
                  pl.BlockSpec(wps.shape, full),
                  pl.BlockSpec(wpm.shape, full),
                  pl.BlockSpec(wpa.shape, full),
                  pl.BlockSpec(wo.shape, full),
                  pl.BlockSpec((1, D_MODEL), full)],
        out_specs=pl.BlockSpec((ts, D_MODEL), row),
        out_shape=jax.ShapeDtypeStruct((t, D_MODEL), F32),
        compiler_params=pltpu.CompilerParams(dimension_semantics=("parallel",),
                                             vmem_limit_bytes=VMEM_LIMIT),
        name="merge",
    )(x2d, ys, ym, ya, gate, wps, wpm, wpa, wo, g)


def _ffn_kernel(x_ref, g1_ref, w1_ref, w2_ref, g2_ref, o_ref):
    x = x_ref[...]
    hb = _rms(x, g1_ref[...]).astype(BF16)
    a = jnp.maximum(_dot(hb, w1_ref[...]), 0.0)
    f = _dot((a * a).astype(BF16), w2_ref[...])
    o_ref[...] = x + _rms(f, g2_ref[...])


def _ffn(x2d, g1, w1, w2, g2, ts):
    t = x2d.shape[0]
    full = lambda i: (0, 0)
    return pl.pallas_call(
        _ffn_kernel,
        grid=(t // ts,),
        in_specs=[pl.BlockSpec((ts, D_MODEL), lambda i: (i, 0)),
                  pl.BlockSpec((1, D_MODEL), full),
                  pl.BlockSpec((D_MODEL, D_FF), full, pipeline_mode=pl.Buffered(1)),
                  pl.BlockSpec((D_FF, D_MODEL), full, pipeline_mode=pl.Buffered(1)),
                  pl.BlockSpec((1, D_MODEL), full)],
        out_specs=pl.BlockSpec((ts, D_MODEL), lambda i: (i, 0)),
        out_shape=jax.ShapeDtypeStruct((t, D_MODEL), F32),
        compiler_params=pltpu.CompilerParams(dimension_semantics=("parallel",),
                                             vmem_limit_bytes=VMEM_LIMIT),
        name="ffn",
    )(x2d, g1, w1, w2, g2)


def _split_w_in(w_in):
    n_if = 2 * MLSTM_HEADS
    c_if = SSM_WIDTH + 4 * MLSTM_WIDTH
    assert c_if == C_AQ and w_in.shape[-1] - n_if == C_GIF
    wa = w_in[..., :c_if].astype(BF16)
    wb = w_in[..., c_if + n_if:].astype(BF16)
    wg = jnp.pad(w_in[..., c_if:c_if + n_if], ((0, 0), (0, 0), (0, LANES - n_if))).astype(BF16)
    return wa, wb, wg


def kernel(x, w_in, conv_w, ssm_a_re, ssm_a_im, ssm_log_dt, ssm_b_re, ssm_b_im, ssm_c_re, ssm_c_im, ssm_d, ssm_w_glu, mlstm_i_bias, mlstm_f_bias, mlstm_head_gain, rel_bias, w_ssm_proj, w_mlstm_proj, w_moba_proj, w_out, w_ff1, w_ff2, norm_mix_pre, norm_mix_post, norm_ffn_pre, norm_ffn_post):
    bsz, seq, _ = x.shape
    depth = w_in.shape[0]
    assert bsz == SUBLANES and seq % MOBA_BLOCK == 0 and seq // MOBA_BLOCK <= SUBLANES
    ts = 256
    tc = 64
    x2d = x.reshape(bsz * seq, D_MODEL)
    wa, wb, wg = _split_w_in(w_in)
    wps, wpm, wpa = w_ssm_proj.astype(BF16), w_mlstm_proj.astype(BF16), w_moba_proj.astype(BF16)
    wo = w_out.astype(BF16)
    w1 = w_ff1.astype(BF16)
    w2 = w_ff2.astype(BF16)
    wglu = ssm_w_glu.astype(BF16)
    gate_bias = jnp.pad(jnp.concatenate([mlstm_i_bias, mlstm_f_bias], axis=-1).astype(F32),
                        ((0, 0), (0, LANES - 2 * MLSTM_HEADS)))
    tiles, far = _moba_bias_tiles(rel_bias)
    for l in range(depth):
        u, qk, mv, mo, aq, ak, av, gate, gif = _inproj(x2d, norm_mix_pre[l][None, :], wa[l], wb[l], wg[l], ts)
        bblk, cblk, ar_b, ai_b = _ssm_params(ssm_a_re[l], ssm_a_im[l], ssm_log_dt[l], ssm_b_re[l], ssm_b_im[l],
                                             ssm_c_re[l], ssm_c_im[l], bsz)
        y_ssm = _ssm(u.reshape(bsz, seq, SSM_WIDTH), bblk, cblk, ar_b, ai_b, ssm_d[l][None, :], wglu[l],
                     bsz, seq, tc)
        y_mlstm = _mlstm(qk, mv, mo, gif, conv_w[l].astype(F32), gate_bias[l][None, :],
                         mlstm_head_gain[l][None, :].astype(F32), bsz, seq)
        y_moba = _moba(aq, ak, av, tiles, far, bsz, seq)
        x2d = _merge(x2d, y_ssm.reshape(bsz * seq, SSM_WIDTH), y_mlstm, y_moba, gate, wps[l], wpm[l], wpa[l],
                     wo[l], norm_mix_post[l][None, :], ts)
        x2d = _ffn(x2d, norm_ffn_pre[l][None, :], w1[l], w2[l], norm_ffn_post[l][None, :], ts)
    return x2d.reshape(bsz, seq, D_MODEL)
```

```python
import functools
import math

import jax
import jax.numpy as jnp
from jax import lax
from jax.experimental import pallas as pl
from jax.experimental.pallas import tpu as pltpu

F32 = jnp.float32
BF16 = jnp.bfloat16

D_MODEL = 1024
SSM_WIDTH = 512
SSM_GROUP = 16
SSM_GROUPS = 32
SSM_STATE = 64
MLSTM_WIDTH = 512
MLSTM_HEADS = 4
MLSTM_HEAD_DIM = 128
MLSTM_CHUNK = 128
CONV_WIDTH = 4
MOBA_WIDTH = 512
MOBA_HEADS = 8
MOBA_HEAD_DIM = 64
MOBA_BLOCK = 256
MOBA_TOPK = 3
REL_BUCKETS = 32
REL_MAX_DIST = 128
D_FF = 4 * D_MODEL
RMS_EPS = 1e-6
NEG_INF = -1e30

LANES = 128
SUBLANES = 8
VMEM_LIMIT = 56 * 1024 * 1024

C_U = 0
C_QK = 512
C_MV = 1536
C_MO = 2048
C_AQ = 2560
C_AK = 3072
C_AV = 3584
C_GATE = 4096
C_GIF = 7168
C_END = 7296


def _rms(x, g):
    return x * lax.rsqrt(jnp.mean(x * x, axis=-1, keepdims=True) + RMS_EPS) * g


def _sigmoid(x):
    return 1.0 / (1.0 + jnp.exp(-x))


def _dot(a, b):
    return jnp.dot(a, b, preferred_element_type=F32)


def _dot_nt(a, b):
    return lax.dot_general(a, b, (((1,), (1,)), ((), ())), preferred_element_type=F32)


def _split3(a):
    hi = a.astype(BF16)
    r1 = a - hi.astype(F32)
    mid = r1.astype(BF16)
    lo = (r1 - mid.astype(F32)).astype(BF16)
    return hi, mid, lo


def _inproj_kernel(x_ref, g_ref, wa_ref, wb_ref, wg_ref, u_ref, qk_ref, mv_ref, mo_ref, aq_ref, ak_ref, av_ref,
                   gate_ref, gif_ref):
    hb = _rms(x_ref[...], g_ref[...]).astype(BF16)

    def proj(w_ref, a, b):
        return _dot(hb, w_ref[:, a:b])

    u_ref[...] = proj(wa_ref, C_U, C_QK)
    qk_ref[...] = proj(wa_ref, C_QK, C_MV)
    mv_ref[...] = proj(wa_ref, C_MV, C_MO)
    mo_ref[...] = proj(wa_ref, C_MO, C_AQ)
    aq_ref[...] = proj(wb_ref, C_AQ - C_AQ, C_AK - C_AQ)
    ak_ref[...] = proj(wb_ref, C_AK - C_AQ, C_AV - C_AQ)
    av_ref[...] = proj(wb_ref, C_AV - C_AQ, C_GATE - C_AQ)
    gate_ref[...] = _sigmoid(proj(wb_ref, C_GATE - C_AQ, C_GIF - C_AQ)).astype(BF16)
    gif_ref[...] = _dot(hb, wg_ref[...])


def _inproj(x2d, g, wa, wb, wg, ts):
    t = x2d.shape[0]
    row = lambda i: (i, 0)
    full = lambda i: (0, 0)

    def out(n, dtype=F32):
        return jax.ShapeDtypeStruct((t, n), dtype), pl.BlockSpec((ts, n), row)

    outs = [out(512), out(1024), out(512), out(512), out(512), out(512), out(512), out(3 * D_MODEL, BF16),
            out(LANES)]
    return pl.pallas_call(
        _inproj_kernel,
        grid=(t // ts,),
        in_specs=[pl.BlockSpec((ts, D_MODEL), row),
                  pl.BlockSpec((1, D_MODEL), full),
                  pl.BlockSpec(wa.shape, full, pipeline_mode=pl.Buffered(1)),
                  pl.BlockSpec(wb.shape, full, pipeline_mode=pl.Buffered(1)),
                  pl.BlockSpec(wg.shape, full, pipeline_mode=pl.Buffered(1))],
        out_specs=[o[1] for o in outs],
        out_shape=[o[0] for o in outs],
        compiler_params=pltpu.CompilerParams(dimension_semantics=("parallel",),
                                             vmem_limit_bytes=VMEM_LIMIT),
        name="inproj",
    )(x2d, g, wa, wb, wg)


SSM_KB = 2
SSM_KB_CH = SSM_WIDTH // SSM_KB
SSM_KB_ST = SSM_KB_CH // SSM_GROUP * SSM_STATE
SSM_SLAB = 512


def _gelu(x):
    return 0.5 * x * (1.0 + jnp.tanh(math.sqrt(2.0 / math.pi) * (x + 0.044715 * (x * x * x))))


def _ssm_kernel(u_ref, perm_ref, perm_t_ref, bblk_ref, cblk_ref, ar_ref, ai_ref, d_ref, wglu_ref, o_ref,
                xs_ref, st_ref, *, bsz, tc):
    @pl.when(pl.program_id(0) == 0)
    def _():
        st_ref[...] = jnp.zeros_like(st_ref)

    u_bm = u_ref[...].reshape(bsz * tc, SSM_WIDTH)
    u_hi = u_bm.astype(BF16)
    u_lo = (u_bm - u_hi.astype(F32)).astype(BF16)
    u_tm_hi = _dot(perm_ref[...], u_hi)
    u = u_tm_hi + _dot(perm_ref[...], u_lo)
    ub = u_tm_hi.astype(BF16)
    for kb in range(SSM_KB):
        xs_ref[:, kb * 2 * SSM_KB_ST:(kb + 1) * 2 * SSM_KB_ST] = _dot(
            ub[:, kb * SSM_KB_CH:(kb + 1) * SSM_KB_CH], bblk_ref[kb])

    for kb in range(SSM_KB):
        for j in range(SSM_KB_ST // SSM_SLAB):
            re0 = kb * 2 * SSM_KB_ST + j * SSM_SLAB
            im0 = re0 + SSM_KB_ST
            co = kb * SSM_KB_ST + j * SSM_SLAB
            ar = ar_ref[:, co:co + SSM_SLAB]
            ai = ai_ref[:, co:co + SSM_SLAB]

            def body(t, carry, re0=re0, im0=im0, ar=ar, ai=ai):
                xr, xi = carry
                r0 = pl.multiple_of(t * bsz, bsz)
                br = xs_ref[pl.ds(r0, bsz), re0:re0 + SSM_SLAB]
                bi = xs_ref[pl.ds(r0, bsz), im0:im0 + SSM_SLAB]
                nr = ar * xr - ai * xi + br
                ni = ar * xi + ai * xr + bi
                xs_ref[pl.ds(r0, bsz), re0:re0 + SSM_SLAB] = nr
                xs_ref[pl.ds(r0, bsz), im0:im0 + SSM_SLAB] = ni
                return nr, ni

            xr, xi = lax.fori_loop(0, tc, body,
                                   (st_ref[:, re0:re0 + SSM_SLAB], st_ref[:, im0:im0 + SSM_SLAB]),
                                   unroll=8)
            st_ref[:, re0:re0 + SSM_SLAB] = xr
            st_ref[:, im0:im0 + SSM_SLAB] = xi

    ys = []
    for kb in range(SSM_KB):
        xb = xs_ref[:, kb * 2 * SSM_KB_ST:(kb + 1) * 2 * SSM_KB_ST].astype(BF16)
        ys.append(_dot(xb, cblk_ref[kb]))
    y = jnp.concatenate(ys, axis=-1) + d_ref[...] * u
    yg = _gelu(y)
    out_tm = (yg * _sigmoid(_dot(yg.astype(BF16), wglu_ref[...]))).astype(BF16)
    o_ref[...] = _dot(perm_t_ref[...], out_tm).astype(BF16).reshape(bsz, tc, SSM_WIDTH)


def _ssm(u3, bblk, cblk, ar, ai, d, wglu, bsz, seq, tc):
    rows = tc * bsz
    full2 = lambda c: (0, 0)
    full3 = lambda c: (0, 0, 0)
    r = jnp.arange(rows)
    perm = (r[None, :] == (r[:, None] % bsz) * tc + r[:, None] // bsz).astype(BF16)
    return pl.pallas_call(
        functools.partial(_ssm_kernel, bsz=bsz, tc=tc),
        grid=(seq // tc,),
        in_specs=[pl.BlockSpec((bsz, tc, SSM_WIDTH), lambda c: (0, c, 0)),
                  pl.BlockSpec((rows, rows), full2),
                  pl.BlockSpec((rows, rows), full2),
                  pl.BlockSpec(bblk.shape, full3),
                  pl.BlockSpec(cblk.shape, full3),
                  pl.BlockSpec(ar.shape, full2),
                  pl.BlockSpec(ai.shape, full2),
                  pl.BlockSpec((1, SSM_WIDTH), full2),
                  pl.BlockSpec((SSM_WIDTH, SSM_WIDTH), full2)],
        out_specs=pl.BlockSpec((bsz, tc, SSM_WIDTH), lambda c: (0, c, 0)),
        out_shape=jax.ShapeDtypeStruct((bsz, seq, SSM_WIDTH), BF16),
        scratch_shapes=[pltpu.VMEM((rows, SSM_KB * 2 * SSM_KB_ST), F32),
                        pltpu.VMEM((bsz, SSM_KB * 2 * SSM_KB_ST), F32)],
        compiler_params=pltpu.CompilerParams(dimension_semantics=("arbitrary",),
                                             vmem_limit_bytes=VMEM_LIMIT),
        name="ssm",
    )(u3, perm, perm.T, bblk, cblk, ar, ai, d, wglu)


def _ssm_params(a_re, a_im, log_dt, b_re, b_im, c_re, c_im, bsz):
    ar, ai = a_re.astype(F32), a_im.astype(F32)
    dt = jnp.exp(log_dt.astype(F32))[:, None]
    decay = jnp.exp(dt * ar)
    abar_r, abar_i = decay * jnp.cos(dt * ai), decay * jnp.sin(dt * ai)
    den = ar * ar + ai * ai
    nr, ni = abar_r - 1.0, abar_i
    fr, fi = (nr * ar + ni * ai) / den, (ni * ar - nr * ai) / den
    br, bi = b_re.astype(F32), b_im.astype(F32)
    bbar_r = fr[..., None] * br - fi[..., None] * bi
    bbar_i = fr[..., None] * bi + fi[..., None] * br
    gl = SSM_GROUPS // SSM_KB
    eye = jnp.eye(gl, dtype=F32)

    def pack_b(bb):
        bb = bb.reshape(SSM_KB, gl, SSM_STATE, SSM_GROUP)
        dense = jnp.einsum('kgph,gq->kghqp', bb, eye)
        return dense.reshape(SSM_KB, gl * SSM_GROUP, gl * SSM_STATE)

    def pack_c(cc):
        cc = cc.reshape(SSM_KB, gl, SSM_GROUP, SSM_STATE)
        dense = jnp.einsum('kghp,gq->kgpqh', cc, eye)
        return dense.reshape(SSM_KB, gl * SSM_STATE, gl * SSM_GROUP)

    bblk = jnp.concatenate([pack_b(bbar_r), pack_b(bbar_i)], axis=-1).astype(BF16)
    cblk = jnp.concatenate([pack_c(c_re.astype(F32)), pack_c(-c_im.astype(F32))], axis=1).astype(BF16)
    ar_b = jnp.broadcast_to(abar_r.reshape(1, -1), (bsz, SSM_GROUPS * SSM_STATE))
    ai_b = jnp.broadcast_to(abar_i.reshape(1, -1), (bsz, SSM_GROUPS * SSM_STATE))
    return bblk, cblk, ar_b, ai_b


def _mlstm_kernel(qk_ref, v_ref, o_ref, gif_ref, convw_ref, bias_ref, gain_ref, tri_ref, spread_ref, out_ref,
                  xp_ref, ct_ref, n_ref, m_ref):
    lc, dh, nh = MLSTM_CHUNK, MLSTM_HEAD_DIM, MLSTM_HEADS
    halo = SUBLANES

    @pl.when(pl.program_id(1) == 0)
    def _():
        xp_ref[0:halo, :] = jnp.zeros((halo, 2 * MLSTM_WIDTH), F32)
        ct_ref[...] = jnp.zeros_like(ct_ref)
        n_ref[...] = jnp.zeros_like(n_ref)
        m_ref[...] = jnp.zeros_like(m_ref)

    xp_ref[halo:halo + lc, :] = qk_ref[...]
    conv = convw_ref[CONV_WIDTH - 1:CONV_WIDTH, :] * xp_ref[halo:halo + lc, :]
    for j in range(CONV_WIDTH - 1):
        off = halo - (CONV_WIDTH - 1) + j
        conv = conv + convw_ref[j:j + 1, :] * xp_ref[off:off + lc, :]
    xp_ref[0:halo, :] = xp_ref[lc:lc + halo, :]
    qk = conv * _sigmoid(conv)

    gpre = gif_ref[...] + bias_ref[...]
    lsig = jnp.minimum(gpre, 0.0) - jnp.log(1.0 + jnp.exp(-jnp.abs(gpre)))
    row_i = lax.broadcasted_iota(jnp.int32, (lc, lc), 0)
    col_i = lax.broadcasted_iota(jnp.int32, (lc, lc), 1)
    causal = col_i <= row_i
    tri = tri_ref[0]
    tri_t = tri_ref[1]
    l_hi, l_mid, l_lo = _split3(lsig)
    bcum_c = _dot(tri, l_hi) + _dot(tri, l_mid) + _dot(tri, l_lo)
    lsig_t = lsig.T
    t_hi, t_mid, t_lo = _split3(lsig_t)
    bcum_r = _dot(t_hi, tri_t) + _dot(t_mid, tri_t) + _dot(t_lo, tri_t)
    gpre_t = gpre.T
    per_time = jnp.where(col_i < nh, gpre, bcum_c)
    spread = spread_ref[...]
    p_hi, p_mid, p_lo = _split3(per_time)
    rep = _dot(p_hi, spread) + _dot(p_mid, spread) + _dot(p_lo, spread)

    hs = range(nh)
    qf = [qk[:, h * dh:(h + 1) * dh] for h in hs]
    q = [x.astype(BF16) for x in qf]
    k = [qk[:, MLSTM_WIDTH + h * dh:MLSTM_WIDTH + (h + 1) * dh] * (dh ** -0.5) for h in hs]
    kb = [x.astype(BF16) for x in k]
    v = [v_ref[:, h * dh:(h + 1) * dh] for h in hs]
    bc = [rep[:, (nh + h) * LANES:(nh + h + 1) * LANES] for h in hs]
    brow = [bcum_r[nh + h:nh + h + 1, :] for h in hs]
    ic = [rep[:, h * LANES:(h + 1) * LANES] for h in hs]
    irow = [gpre_t[h:h + 1, :] for h in hs]
    gtot = [x[lc - 1:lc, :] for x in bc]
    m_prev = [m_ref[h:h + 1, :] for h in hs]
    n_prev = [n_ref[h:h + 1, :] for h in hs]
    ct_prev = [ct_ref[h] for h in hs]

    qk_t = [_dot_nt(q[h], kb[h]) for h in hs]
    q_ct = [_dot(q[h], ct_prev[h].astype(BF16)) for h in hs]
    m_loc = [jnp.max(gtot[h] - brow[h] + irow[h], axis=-1, keepdims=True) for h in hs]
    wgt_c = [jnp.exp(gtot[h] - bc[h] + ic[h] - m_loc[h]) for h in hs]
    d_ct = [_dot(k[h].T.astype(BF16), (wgt_c[h] * v[h]).astype(BF16)) for h in hs]

    log_d = [jnp.where(causal, bc[h] - brow[h] + irow[h], NEG_INF) for h in hs]
    m_intra = [jnp.max(log_d[h], axis=-1, keepdims=True) for h in hs]
    qn = [jnp.sum(qf[h] * n_prev[h], axis=-1, keepdims=True) for h in hs]
    log_inter = [bc[h] + m_prev[h] for h in hs]
    m_q = [jnp.maximum(log_inter[h], m_intra[h]) for h in hs]
    s_mat = [qk_t[h] * jnp.exp(log_d[h] - m_q[h]) for h in hs]
    s_sum = [jnp.sum(s_mat[h], axis=-1, keepdims=True) for h in hs]
    s_v = [_dot(s_mat[h].astype(BF16), v[h].astype(BF16)) for h in hs]
    inter = [jnp.exp(log_inter[h] - m_q[h]) for h in hs]
    hh = [(inter[h] * q_ct[h] + s_v[h])
          / jnp.maximum(jnp.abs(s_sum[h] + inter[h] * qn[h]), jnp.exp(-m_q[h])) for h in hs]
    ms = [jnp.mean(hh[h] * hh[h], axis=-1, keepdims=True) for h in hs]
    for h in hs:
        og = _sigmoid(o_ref[:, h * dh:(h + 1) * dh])
        out_ref[:, h * dh:(h + 1) * dh] = og * (hh[h] * lax.rsqrt(ms[h] + RMS_EPS)
                                                * gain_ref[:, h * dh:(h + 1) * dh])

    for h in hs:
        d_n = jnp.sum(wgt_c[h] * k[h], axis=0, keepdims=True)
        m_new = jnp.maximum(gtot[h] + m_prev[h], m_loc[h])
        a = jnp.exp(gtot[h] + m_prev[h] - m_new)
        bb = jnp.exp(m_loc[h] - m_new)
        ct_ref[h] = a * ct_prev[h] + bb * d_ct[h]
        n_ref[h:h + 1, :] = a * n_prev[h] + bb * d_n
        m_ref[h:h + 1, :] = m_new


def _mlstm(qk, mv, mo, gif, convw, bias, gain, bsz, seq):
    t = bsz * seq
    lc = MLSTM_CHUNK
    nc = seq // lc
    row = lambda b, c: (b * nc + c, 0)
    full = lambda b, c: (0, 0)
    r = jnp.arange(lc)
    lower = r[None, :] <= r[:, None]
    tri = jnp.stack([lower, lower.T]).astype(BF16)
    spread = (r[:, None] == jnp.arange(2 * MLSTM_HEADS * LANES)[None, :] // LANES).astype(BF16)
    return pl.pallas_call(
        _mlstm_kernel,
        grid=(bsz, nc),
        in_specs=[pl.BlockSpec((lc, 2 * MLSTM_WIDTH), row),
                  pl.BlockSpec((lc, MLSTM_WIDTH), row),
                  pl.BlockSpec((lc, MLSTM_WIDTH), row),
                  pl.BlockSpec((lc, LANES), row),
                  pl.BlockSpec((CONV_WIDTH, 2 * MLSTM_WIDTH), full),
                  pl.BlockSpec((1, LANES), full),
                  pl.BlockSpec((1, MLSTM_WIDTH), full),
                  pl.BlockSpec(tri.shape, lambda b, c: (0, 0, 0)),
                  pl.BlockSpec(spread.shape, full)],
        out_specs=pl.BlockSpec((lc, MLSTM_WIDTH), row),
        out_shape=jax.ShapeDtypeStruct((t, MLSTM_WIDTH), F32),
        scratch_shapes=[pltpu.VMEM((lc + SUBLANES, 2 * MLSTM_WIDTH), F32),
                        pltpu.VMEM((MLSTM_HEADS, MLSTM_HEAD_DIM, MLSTM_HEAD_DIM), F32),
                        pltpu.VMEM((SUBLANES, MLSTM_HEAD_DIM), F32),
                        pltpu.VMEM((SUBLANES, LANES), F32)],
        compiler_params=pltpu.CompilerParams(dimension_semantics=("parallel", "arbitrary"),
                                             vmem_limit_bytes=VMEM_LIMIT),
        name="mlstm",
    )(qk, mv, mo, gif, convw, bias, gain, tri, spread)


def _rel_bucket(dist):
    max_exact = REL_BUCKETS // 2
    is_small = dist < max_exact
    large = max_exact + (jnp.log(jnp.maximum(dist, 1).astype(F32) / max_exact)
                         / math.log(REL_MAX_DIST / max_exact) * (REL_BUCKETS - max_exact)).astype(jnp.int32)
    large = jnp.minimum(large, REL_BUCKETS - 1)
    return jnp.where(is_small, dist, large)


LOG2E = math.log2(math.e)
MOBA_LOOKAHEAD = 8


def _moba_bias_tiles(rel_bias):
    bs = MOBA_BLOCK
    d_own = jnp.arange(bs)[None, :] - jnp.arange(bs)[:, None]
    tab = rel_bias.astype(F32) * LOG2E
    buckets = jnp.stack([_rel_bucket(jnp.maximum(d_own, 0)), _rel_bucket(bs + d_own)])
    onehot = (buckets[..., None] == jnp.arange(REL_BUCKETS)).astype(F32)
    tiles = jnp.einsum('tjib,hb->htji', onehot, tab, precision=lax.Precision.HIGHEST)
    causal = jnp.stack([d_own >= 0, jnp.ones_like(d_own, dtype=bool)])
    tiles = jnp.where(causal[None], tiles, NEG_INF)
    far = jnp.take(tab, _rel_bucket(jnp.full((1,), 2 * bs, jnp.int32)), axis=1)
    far_rows = jnp.broadcast_to(far[:, None, :], (MOBA_HEADS, SUBLANES, LANES)).reshape(-1, LANES)
    return tiles, jnp.pad(far_rows, ((0, LANES - MOBA_HEADS * SUBLANES), (0, 0)))


def _moba_kernel(q_ref, k_ref, v_ref, tiles_ref, far_ref, out_ref,
                 kaug_ref, vt_ref, km_ref, qaug_ref, *, nb):
    bs, nh, dh = MOBA_BLOCK, MOBA_HEADS, MOBA_HEAD_DIM
    qi = pl.program_id(1)

    @pl.when(qi == 0)
    def _():
        lane = lax.broadcasted_iota(jnp.int32, (bs, LANES), 1)
        means = []
        for n in range(nb):
            kblk = k_ref[n * bs:(n + 1) * bs, :]
            means.append(jnp.mean(kblk, axis=0, keepdims=True))
            for h in range(nh):
                p, e = divmod(h, 2)
                o = (1 - e) * dh
                hot = jnp.where(lane == o + n, 1.0, jnp.where(lane == o + SUBLANES + n, 1.0, 0.0))
                kaug_ref[h, n] = jnp.where(lane // dh == e, kblk[:, p * LANES:(p + 1) * LANES], hot).astype(BF16)
            vt_ref[n] = v_ref[n * bs:(n + 1) * bs, :].T.astype(BF16)
        means += [jnp.zeros_like(means[0])] * (SUBLANES - nb)
        kmean = jnp.concatenate(means, axis=0)
        rows = lax.broadcasted_iota(jnp.int32, (LANES, MOBA_WIDTH), 0)
        cols = lax.broadcasted_iota(jnp.int32, (LANES, MOBA_WIDTH), 1)
        tiled = jnp.concatenate([kmean] * (LANES // SUBLANES), axis=0)
        km_ref[...] = jnp.where(rows // SUBLANES == cols // dh, tiled, 0.0)

    q = q_ref[...]
    q_hi, q_mid, _ = _split3(q)
    m_hi, m_mid, _ = _split3(km_ref[...])
    gate = _dot_nt(m_hi, q_hi) + _dot_nt(m_mid, q_hi) + _dot_nt(m_hi, q_mid)
    nrows = LANES
    blk = lax.broadcasted_iota(jnp.int32, (nrows, bs), 0) % SUBLANES
    gate = jnp.where(blk < qi, gate, NEG_INF)
    rank = jnp.zeros((nrows, bs), F32)
    for s in range(1, SUBLANES):
        fwd = pltpu.roll(gate, nrows - s, 0)
        back = pltpu.roll(gate, SUBLANES - s, 0)
        wrapped = blk + s >= SUBLANES
        other = jnp.where(wrapped, back, fwd)
        rank = rank + jnp.where(wrapped, jnp.where(other >= gate, 1.0, 0.0), jnp.where(other > gate, 1.0, 0.0))
    chosen = rank < MOBA_TOPK
    far = jnp.concatenate([far_ref[...]] * (bs // LANES), axis=1)
    far_hi = far.astype(BF16).astype(F32)
    far_lo = far - far_hi
    sel_hi = jnp.where(blk < qi - 1, jnp.where(chosen, far_hi, NEG_INF),
                       jnp.where(blk == qi - 1, jnp.where(chosen, 0.0, NEG_INF),
                                 jnp.where(blk == qi, 0.0, NEG_INF)))
    sel_lo = jnp.where(blk < qi - 1, jnp.where(chosen, far_lo, 0.0), 0.0)

    q_t = (q * (dh ** -0.5 * LOG2E)).T
    pad = jnp.zeros((dh - 2 * SUBLANES, bs), F32)
    for h in range(nh):
        e = h % 2
        extra = [sel_hi[h * SUBLANES:(h + 1) * SUBLANES], sel_lo[h * SUBLANES:(h + 1) * SUBLANES], pad]
        q_h = [q_t[h * dh:(h + 1) * dh]]
        qaug_ref[h] = jnp.concatenate(q_h + extra if e == 0 else extra + q_h, axis=0).astype(BF16)

    def scores(h, n):
        return _dot(kaug_ref[h, n], qaug_ref[h])

    def absorb(h, n, s, state):
        mx = jnp.max(s, axis=0, keepdims=True)
        v_t = vt_ref[n, h * dh:(h + 1) * dh, :]
        m_old, l_old, acc = state
        m_new = jnp.maximum(m_old, mx)
        alpha = jnp.exp2(m_old - m_new)
        pr = jnp.exp2(s - m_new)
        return (m_new, alpha * l_old + jnp.sum(pr, axis=0, keepdims=True),
                alpha * acc + _dot(v_t, pr.astype(BF16)))

    def run_tiles(jobs, states):
        states = list(states)
        pending = {}
        for i in range(min(MOBA_LOOKAHEAD, len(jobs))):
            pending[i] = scores(*jobs[i][:2])
        for i, (h, n, tile) in enumerate(jobs):
            s = pending.pop(i)
            if tile is not None:
                s = tile + s
            states[h] = absorb(h, n, s, states[h])
            if i + MOBA_LOOKAHEAD < len(jobs):
                pending[i + MOBA_LOOKAHEAD] = scores(*jobs[i + MOBA_LOOKAHEAD][:2])
        return tuple(states)

    states = tuple((jnp.full((1, bs), NEG_INF, F32), jnp.zeros((1, bs), F32), jnp.zeros((dh, bs), F32))
                   for _ in range(nh))

    def near_body(k, carry):
        return run_tiles([(h, qi - k, tiles_ref[h, k]) for h in range(nh)], carry)

    def far_body(n, carry):
        return run_tiles([(h, n, None) for h in range(nh)], carry)

    states = lax.fori_loop(0, jnp.minimum(qi + 1, 2), near_body, states)
    states = lax.fori_loop(0, jnp.maximum(qi - 1, 0), far_body, states)
    out_t = jnp.concatenate([acc * (1.0 / l) for (_, l, acc) in states], axis=0)
    out_ref[...] = out_t.T


def _moba(aq, ak, av, tiles, far, bsz, seq):
    t = bsz * seq
    bs = MOBA_BLOCK
    nb = seq // bs
    return pl.pallas_call(
        functools.partial(_moba_kernel, nb=nb),
        grid=(bsz, nb),
        in_specs=[pl.BlockSpec((bs, MOBA_WIDTH), lambda b, i: (b * nb + i, 0)),
                  pl.BlockSpec((seq, MOBA_WIDTH), lambda b, i: (b, 0)),
                  pl.BlockSpec((seq, MOBA_WIDTH), lambda b, i: (b, 0)),
                  pl.BlockSpec(tiles.shape, lambda b, i: (0, 0, 0, 0)),
                  pl.BlockSpec(far.shape, lambda b, i: (0, 0))],
        out_specs=pl.BlockSpec((bs, MOBA_WIDTH), lambda b, i: (b * nb + i, 0)),
        out_shape=jax.ShapeDtypeStruct((t, MOBA_WIDTH), F32),
        scratch_shapes=[pltpu.VMEM((MOBA_HEADS, nb, bs, LANES), BF16),
                        pltpu.VMEM((nb, MOBA_WIDTH, bs), BF16),
                        pltpu.VMEM((LANES, MOBA_WIDTH), F32),
                        pltpu.VMEM((MOBA_HEADS, LANES, bs), BF16)],
        compiler_params=pltpu.CompilerParams(dimension_semantics=("parallel", "arbitrary"),
                                             vmem_limit_bytes=VMEM_LIMIT),
        name="moba",
    )(aq, ak, av, tiles, far)


def _merge_kernel(x_ref, ys_ref, ym_ref, ya_ref, gate_ref, wps_ref, wpm_ref, wpa_ref, wo_ref, g_ref, o_ref):
    def gate(i):
        return gate_ref[:, i * D_MODEL:(i + 1) * D_MODEL].astype(F32)

    merged = (gate(0) * _dot(ys_ref[...].astype(BF16), wps_ref[...])
              + gate(1) * _dot(ym_ref[...].astype(BF16), wpm_ref[...])
              + gate(2) * _dot(ya_ref[...].astype(BF16), wpa_ref[...]))
    z = _dot(merged.astype(BF16), wo_ref[...])
    o_ref[...] = x_ref[...] + _rms(z, g_ref[...])


def _merge(x2d, ys, ym, ya, gate, wps, wpm, wpa, wo, g, ts):
    t = x2d.shape[0]
    row = lambda i: (i, 0)
    full = lambda i: (0, 0)
    return pl.pallas_call(
        _merge_kernel,
        grid=(t // ts,),
        in_specs=[pl.BlockSpec((ts, D_MODEL), row),
                  pl.BlockSpec((ts, SSM_WIDTH), row),
                  pl.BlockSpec((ts, MLSTM_WIDTH), row),
                  pl.BlockSpec((ts, MOBA_WIDTH), row),
                  pl.BlockSpec((ts, 3 * D_MODEL), row),
                  pl.BlockSpec(wps.shape, full),
                  pl.BlockSpec(wpm.shape, full),
                  pl.BlockSpec(wpa.shape, full),
                  pl.BlockSpec(wo.shape, full),
                  pl.BlockSpec((1, D_MODEL), full)],
        out_specs=pl.BlockSpec((ts, D_MODEL), row),
        out_shape=jax.ShapeDtypeStruct((t, D_MODEL), F32),
        compiler_params=pltpu.CompilerParams(dimension_semantics=("parallel",),
                                             vmem_limit_bytes=VMEM_LIMIT),
        name="merge",
    )(x2d, ys, ym, ya, gate, wps, wpm, wpa, wo, g)


def _ffn_kernel(x_ref, g1_ref, w1_ref, w2_ref, g2_ref, o_ref):
    x = x_ref[...]
    hb = _rms(x, g1_ref[...]).astype(BF16)
    a = jnp.maximum(_dot(hb, w1_ref[...]), 0.0)
    f = _dot((a * a).astype(BF16), w2_ref[...])
    o_ref[...] = x + _rms(f, g2_ref[...])


def _ffn(x2d, g1, w1, w2, g2, ts):
    t = x2d.shape[0]
    full = lambda i: (0, 0)
    return pl.pallas_call(
        _ffn_kernel,
        grid=(t // ts,),
        in_specs=[pl.BlockSpec((ts, D_MODEL), lambda i: (i, 0)),
                  pl.BlockSpec((1, D_MODEL), full),
                  pl.BlockSpec((D_MODEL, D_FF), full, pipeline_mode=pl.Buffered(1)),
                  pl.BlockSpec((D_FF, D_MODEL), full, pipeline_mode=pl.Buffered(1)),
                  pl.BlockSpec((1, D_MODEL), full)],
        out_specs=pl.BlockSpec((ts, D_MODEL), lambda i: (i, 0)),
        out_shape=jax.ShapeDtypeStruct((t, D_MODEL), F32),
        compiler_params=pltpu.CompilerParams(dimension_semantics=("parallel",),
                                             vmem_limit_bytes=VMEM_LIMIT),
        name="ffn",
    )(x2d, g1, w1, w2, g2)


def _split_w_in(w_in):
    n_if = 2 * MLSTM_HEADS
    c_if = SSM_WIDTH + 4 * MLSTM_WIDTH
    assert c_if == C_AQ and w_in.shape[-1] - n_if == C_GIF
    wa = w_in[..., :c_if].astype(BF16)
    wb = w_in[..., c_if + n_if:].astype(BF16)
    wg = jnp.pad(w_in[..., c_if:c_if + n_if], ((0, 0), (0, 0), (0, LANES - n_if))).astype(BF16)
    return wa, wb, wg


def kernel(x, w_in, conv_w, ssm_a_re, ssm_a_im, ssm_log_dt, ssm_b_re, ssm_b_im, ssm_c_re, ssm_c_im, ssm_d, ssm_w_glu, mlstm_i_bias, mlstm_f_bias, mlstm_head_gain, rel_bias, w_ssm_proj, w_mlstm_proj, w_moba_proj, w_out, w_ff1, w_ff2, norm_mix_pre, norm_mix_post, norm_ffn_pre, norm_ffn_post):
    bsz, seq, _ = x.shape
    depth = w_in.shape[0]
    assert bsz == SUBLANES and seq % MOBA_BLOCK == 0 and seq // MOBA_BLOCK <= SUBLANES
    ts = 512
    tc = 64
    x2d = x.reshape(bsz * seq, D_MODEL)
    wa, wb, wg = _split_w_in(w_in)
    wps, wpm, wpa = w_ssm_proj.astype(BF16), w_mlstm_proj.astype(BF16), w_moba_proj.astype(BF16)
    wo = w_out.astype(BF16)
    w1 = w_ff1.astype(BF16)
    w2 = w_ff2.astype(BF16)
    wglu = ssm_w_glu.astype(BF16)
    gate_bias = jnp.pad(jnp.concatenate([mlstm_i_bias, mlstm_f_bias], axis=-1).astype(F32),
                        ((0, 0), (0, LANES - 2 * MLSTM_HEADS)))
    tiles, far = _moba_bias_tiles(rel_bias)
    for l in range(depth):
        u, qk, mv, mo, aq, ak, av, gate, gif = _inproj(x2d, norm_mix_pre[l][None, :], wa[l], wb[l], wg[l], ts)
        bblk, cblk, ar_b, ai_b = _ssm_params(ssm_a_re[l], ssm_a_im[l], ssm_log_dt[l], ssm_b_re[l], ssm_b_im[l],
                                             ssm_c_re[l], ssm_c_im[l], bsz)
        y_ssm = _ssm(u.reshape(bsz, seq, SSM_WIDTH), bblk, cblk, ar_b, ai_b, ssm_d[l][None, :], wglu[l],
                     bsz, seq, tc)
        y_mlstm = _mlstm(qk, mv, mo, gif, conv_w[l].astype(F32), gate_bias[l][None, :],
                         mlstm_head_gain[l][None, :].astype(F32), bsz, seq)
        y_moba = _moba(aq, ak, av, tiles, far, bsz, seq)
        x2d = _merge(x2d, y_ssm.reshape(bsz * seq, SSM_WIDTH), y_mlstm, y_moba, gate, wps[l], wpm[l], wpa[l],
                     wo[l], norm_mix_post[l][None, :], ts)
        x2d = _ffn(x2d, norm_ffn_pre[l][None, :], w1[l], w2[l], norm_ffn_post[l][None, :], ts)
    return x2d.reshape(bsz, seq, D_MODEL)
```

```python
import functools
import math

import jax
import jax.numpy as jnp
from jax import lax
from jax.experimental import pallas as pl
from jax.experimental.pallas import tpu as pltpu

F32 = jnp.float32
BF16 = jnp.bfloat16

D_MODEL = 1024
SSM_WIDTH = 512
SSM_GROUP = 16
SSM_GROUPS = 32
SSM_STATE = 64
MLSTM_WIDTH = 512
MLSTM_HEADS = 4
MLSTM_HEAD_DIM = 128
MLSTM_CHUNK = 128
MLSTM_CHUNKS_PER_STEP = 4
CONV_WIDTH = 4
MOBA_WIDTH = 512
MOBA_HEADS = 8
MOBA_HEAD_DIM = 64
MOBA_BLOCK = 256
MOBA_TOPK = 3
REL_BUCKETS = 32
REL_MAX_DIST = 128
D_FF = 4 * D_MODEL
RMS_EPS = 1e-6
NEG_INF = -1e30

LANES = 128
SUBLANES = 8
VMEM_LIMIT = 56 * 1024 * 1024

C_U = 0
C_QK = 512
C_MV = 1536
C_MO = 2048
C_AQ = 2560
C_AK = 3072
C_AV = 3584
C_GATE = 4096
C_GIF = 7168
C_END = 7296


def _rms(x, g):
    return x * lax.rsqrt(jnp.mean(x * x, axis=-1, keepdims=True) + RMS_EPS) * g


def _sigmoid(x):
    return 1.0 / (1.0 + jnp.exp(-x))


def _dot(a, b):
    return jnp.dot(a, b, preferred_element_type=F32)


def _dot_nt(a, b):
    return lax.dot_general(a, b, (((1,), (1,)), ((), ())), preferred_element_type=F32)


def _split3(a):
    hi = a.astype(BF16)
    r1 = a - hi.astype(F32)
    mid = r1.astype(BF16)
    lo = (r1 - mid.astype(F32)).astype(BF16)
    return hi, mid, lo


def _inproj_kernel(x_ref, g_ref, wa_ref, wb_ref, wg_ref, u_ref, qk_ref, mv_ref, mo_ref, aq_ref, ak_ref, av_ref,
                   gate_ref, gif_ref):
    hb = _rms(x_ref[...], g_ref[...]).astype(BF16)

    def proj(w_ref, a, b):
        return _dot(hb, w_ref[:, a:b])

    u_ref[...] = proj(wa_ref, C_U, C_QK)
    qk_ref[...] = proj(wa_ref, C_QK, C_MV)
    mv_ref[...] = proj(wa_ref, C_MV, C_MO)
    mo_ref[...] = proj(wa_ref, C_MO, C_AQ)
    aq_ref[...] = proj(wb_ref, C_AQ - C_AQ, C_AK - C_AQ)
    ak_ref[...] = proj(wb_ref, C_AK - C_AQ, C_AV - C_AQ)
    av_ref[...] = proj(wb_ref, C_AV - C_AQ, C_GATE - C_AQ)
    gate_ref[...] = _sigmoid(proj(wb_ref, C_GATE - C_AQ, C_GIF - C_AQ)).astype(BF16)
    gif_ref[...] = _dot(hb, wg_ref[...])


def _inproj(x2d, g, wa, wb, wg, ts):
    t = x2d.shape[0]
    row = lambda i: (i, 0)
    full = lambda i: (0, 0)

    def out(n, dtype=F32):
        return jax.ShapeDtypeStruct((t, n), dtype), pl.BlockSpec((ts, n), row)

    outs = [out(512), out(1024), out(512), out(512), out(512), out(512), out(512), out(3 * D_MODEL, BF16),
            out(LANES)]
    return pl.pallas_call(
        _inproj_kernel,
        grid=(t // ts,),
        in_specs=[pl.BlockSpec((ts, D_MODEL), row),
                  pl.BlockSpec((1, D_MODEL), full),
                  pl.BlockSpec(wa.shape, full, pipeline_mode=pl.Buffered(1)),
                  pl.BlockSpec(wb.shape, full, pipeline_mode=pl.Buffered(1)),
                  pl.BlockSpec(wg.shape, full, pipeline_mode=pl.Buffered(1))],
        out_specs=[o[1] for o in outs],
        out_shape=[o[0] for o in outs],
        compiler_params=pltpu.CompilerParams(dimension_semantics=("parallel",),
                                             vmem_limit_bytes=VMEM_LIMIT),
        name="inproj",
    )(x2d, g, wa, wb, wg)


SSM_KB = 2
SSM_KB_CH = SSM_WIDTH // SSM_KB
SSM_KB_ST = SSM_KB_CH // SSM_GROUP * SSM_STATE
SSM_SLAB = 512


def _gelu(x):
    return 0.5 * x * (1.0 + jnp.tanh(math.sqrt(2.0 / math.pi) * (x + 0.044715 * (x * x * x))))


def _ssm_kernel(u_ref, perm_ref, perm_t_ref, bblk_ref, cblk_ref, ar_ref, ai_ref, d_ref, wglu_ref, o_ref,
                xs_ref, st_ref, *, bsz, tc):
    @pl.when(pl.program_id(0) == 0)
    def _():
        st_ref[...] = jnp.zeros_like(st_ref)

    u_bm = u_ref[...].reshape(bsz * tc, SSM_WIDTH)
    u_hi = u_bm.astype(BF16)
    u_lo = (u_bm - u_hi.astype(F32)).astype(BF16)
    u_tm_hi = _dot(perm_ref[...], u_hi)
    u = u_tm_hi + _dot(perm_ref[...], u_lo)
    ub = u_tm_hi.astype(BF16)
    for kb in range(SSM_KB):
        xs_ref[:, kb * 2 * SSM_KB_ST:(kb + 1) * 2 * SSM_KB_ST] = _dot(
            ub[:, kb * SSM_KB_CH:(kb + 1) * SSM_KB_CH], bblk_ref[kb])

    for kb in range(SSM_KB):
        for j in range(SSM_KB_ST // SSM_SLAB):
            re0 = kb * 2 * SSM_KB_ST + j * SSM_SLAB
            im0 = re0 + SSM_KB_ST
            co = kb * SSM_KB_ST + j * SSM_SLAB
            ar = ar_ref[:, co:co + SSM_SLAB]
            ai = ai_ref[:, co:co + SSM_SLAB]

            def body(t, carry, re0=re0, im0=im0, ar=ar, ai=ai):
                xr, xi = carry
                r0 = pl.multiple_of(t * bsz, bsz)
                br = xs_ref[pl.ds(r0, bsz), re0:re0 + SSM_SLAB]
                bi = xs_ref[pl.ds(r0, bsz), im0:im0 + SSM_SLAB]
                nr = ar * xr - ai * xi + br
                ni = ar * xi + ai * xr + bi
                xs_ref[pl.ds(r0, bsz), re0:re0 + SSM_SLAB] = nr
                xs_ref[pl.ds(r0, bsz), im0:im0 + SSM_SLAB] = ni
                return nr, ni

            xr, xi = lax.fori_loop(0, tc, body,
                                   (st_ref[:, re0:re0 + SSM_SLAB], st_ref[:, im0:im0 + SSM_SLAB]),
                                   unroll=8)
            st_ref[:, re0:re0 + SSM_SLAB] = xr
            st_ref[:, im0:im0 + SSM_SLAB] = xi

    ys = []
    for kb in range(SSM_KB):
        xb = xs_ref[:, kb * 2 * SSM_KB_ST:(kb + 1) * 2 * SSM_KB_ST].astype(BF16)
        ys.append(_dot(xb, cblk_ref[kb]))
    y = jnp.concatenate(ys, axis=-1) + d_ref[...] * u
    yg = _gelu(y)
    out_tm = (yg * _sigmoid(_dot(yg.astype(BF16), wglu_ref[...]))).astype(BF16)
    o_ref[...] = _dot(perm_t_ref[...], out_tm).astype(BF16).reshape(bsz, tc, SSM_WIDTH)


def _ssm(u3, bblk, cblk, ar, ai, d, wglu, bsz, seq, tc):
    rows = tc * bsz
    full2 = lambda c: (0, 0)
    full3 = lambda c: (0, 0, 0)
    r = jnp.arange(rows)
    perm = (r[None, :] == (r[:, None] % bsz) * tc + r[:, None] // bsz).astype(BF16)
    return pl.pallas_call(
        functools.partial(_ssm_kernel, bsz=bsz, tc=tc),
        grid=(seq // tc,),
        in_specs=[pl.BlockSpec((bsz, tc, SSM_WIDTH), lambda c: (0, c, 0)),
                  pl.BlockSpec((rows, rows), full2),
                  pl.BlockSpec((rows, rows), full2),
                  pl.BlockSpec(bblk.shape, full3),
                  pl.BlockSpec(cblk.shape, full3),
                  pl.BlockSpec(ar.shape, full2),
                  pl.BlockSpec(ai.shape, full2),
                  pl.BlockSpec((1, SSM_WIDTH), full2),
                  pl.BlockSpec((SSM_WIDTH, SSM_WIDTH), full2)],
        out_specs=pl.BlockSpec((bsz, tc, SSM_WIDTH), lambda c: (0, c, 0)),
        out_shape=jax.ShapeDtypeStruct((bsz, seq, SSM_WIDTH), BF16),
        scratch_shapes=[pltpu.VMEM((rows, SSM_KB * 2 * SSM_KB_ST), F32),
                        pltpu.VMEM((bsz, SSM_KB * 2 * SSM_KB_ST), F32)],
        compiler_params=pltpu.CompilerParams(dimension_semantics=("arbitrary",),
                                             vmem_limit_bytes=VMEM_LIMIT),
        name="ssm",
    )(u3, perm, perm.T, bblk, cblk, ar, ai, d, wglu)


def _ssm_params(a_re, a_im, log_dt, b_re, b_im, c_re, c_im, bsz):
    ar, ai = a_re.astype(F32), a_im.astype(F32)
    dt = jnp.exp(log_dt.astype(F32))[:, None]
    decay = jnp.exp(dt * ar)
    abar_r, abar_i = decay * jnp.cos(dt * ai), decay * jnp.sin(dt * ai)
    den = ar * ar + ai * ai
    nr, ni = abar_r - 1.0, abar_i
    fr, fi = (nr * ar + ni * ai) / den, (ni * ar - nr * ai) / den
    br, bi = b_re.astype(F32), b_im.astype(F32)
    bbar_r = fr[..., None] * br - fi[..., None] * bi
    bbar_i = fr[..., None] * bi + fi[..., None] * br
    gl = SSM_GROUPS // SSM_KB
    eye = jnp.eye(gl, dtype=F32)

    def pack_b(bb):
        bb = bb.reshape(SSM_KB, gl, SSM_STATE, SSM_GROUP)
        dense = jnp.einsum('kgph,gq->kghqp', bb, eye)
        return dense.reshape(SSM_KB, gl * SSM_GROUP, gl * SSM_STATE)

    def pack_c(cc):
        cc = cc.reshape(SSM_KB, gl, SSM_GROUP, SSM_STATE)
        dense = jnp.einsum('kghp,gq->kgpqh', cc, eye)
        return dense.reshape(SSM_KB, gl * SSM_STATE, gl * SSM_GROUP)

    bblk = jnp.concatenate([pack_b(bbar_r), pack_b(bbar_i)], axis=-1).astype(BF16)
    cblk = jnp.concatenate([pack_c(c_re.astype(F32)), pack_c(-c_im.astype(F32))], axis=1).astype(BF16)
    ar_b = jnp.broadcast_to(abar_r.reshape(1, -1), (bsz, SSM_GROUPS * SSM_STATE))
    ai_b = jnp.broadcast_to(abar_i.reshape(1, -1), (bsz, SSM_GROUPS * SSM_STATE))
    return bblk, cblk, ar_b, ai_b


def _mlstm_kernel(qk_ref, v_ref, o_ref, gif_ref, convw_ref, bias_ref, gain_ref, tri_ref, spread_ref, out_ref,
                  xp_ref, ct_ref, n_ref, m_ref):
    lc, dh, nh = MLSTM_CHUNK, MLSTM_HEAD_DIM, MLSTM_HEADS
    halo = SUBLANES

    @pl.when(pl.program_id(1) == 0)
    def _():
        xp_ref[0:halo, :] = jnp.zeros((halo, 2 * MLSTM_WIDTH), F32)
        ct_ref[...] = jnp.zeros_like(ct_ref)
        n_ref[...] = jnp.zeros_like(n_ref)
        m_ref[...] = jnp.zeros_like(m_ref)

    rows = qk_ref.shape[0]
    xp_ref[halo:halo + rows, :] = qk_ref[...]
    conv = convw_ref[CONV_WIDTH - 1:CONV_WIDTH, :] * xp_ref[halo:halo + rows, :]
    for j in range(CONV_WIDTH - 1):
        off = halo - (CONV_WIDTH - 1) + j
        conv = conv + convw_ref[j:j + 1, :] * xp_ref[off:off + rows, :]
    xp_ref[0:halo, :] = xp_ref[rows:rows + halo, :]
    qk_all = conv * _sigmoid(conv)

    hs = range(nh)
    state = ([m_ref[h:h + 1, :] for h in hs],
             [n_ref[h:h + 1, :] for h in hs],
             [ct_ref[h] for h in hs])
    for c in range(rows // lc):
        r0 = c * lc
        state = _mlstm_chunk(qk_all[r0:r0 + lc], v_ref[r0:r0 + lc, :], o_ref[r0:r0 + lc, :],
                             gif_ref[r0:r0 + lc, :] + bias_ref[...], gain_ref[...], tri_ref, spread_ref[...],
                             state, out_ref.at[r0:r0 + lc, :])
    for h in hs:
        m_ref[h:h + 1, :] = state[0][h]
        n_ref[h:h + 1, :] = state[1][h]
        ct_ref[h] = state[2][h]


def _mlstm_chunk(qk, v_all, o_all, gpre, gain, tri_ref, spread, state, out_ref):
    lc, dh, nh = MLSTM_CHUNK, MLSTM_HEAD_DIM, MLSTM_HEADS
    m_prev, n_prev, ct_prev = state
    lsig = jnp.minimum(gpre, 0.0) - jnp.log(1.0 + jnp.exp(-jnp.abs(gpre)))
    row_i = lax.broadcasted_iota(jnp.int32, (lc, lc), 0)
    col_i = lax.broadcasted_iota(jnp.int32, (lc, lc), 1)
    causal = col_i <= row_i
    tri = tri_ref[0]
    tri_t = tri_ref[1]
    l_hi, l_mid, l_lo = _split3(lsig)
    bcum_c = _dot(tri, l_hi) + _dot(tri, l_mid) + _dot(tri, l_lo)
    lsig_t = lsig.T
    t_hi, t_mid, t_lo = _split3(lsig_t)
    bcum_r = _dot(t_hi, tri_t) + _dot(t_mid, tri_t) + _dot(t_lo, tri_t)
    gpre_t = gpre.T
    per_time = jnp.where(col_i < nh, gpre, bcum_c)
    p_hi, p_mid, p_lo = _split3(per_time)
    rep = _dot(p_hi, spread) + _dot(p_mid, spread) + _dot(p_lo, spread)

    hs = range(nh)
    qf = [qk[:, h * dh:(h + 1) * dh] for h in hs]
    q = [x.astype(BF16) for x in qf]
    k = [qk[:, MLSTM_WIDTH + h * dh:MLSTM_WIDTH + (h + 1) * dh] * (dh ** -0.5) for h in hs]
    kb = [x.astype(BF16) for x in k]
    v = [v_all[:, h * dh:(h + 1) * dh] for h in hs]
    bc = [rep[:, (nh + h) * LANES:(nh + h + 1) * LANES] for h in hs]
    brow = [bcum_r[nh + h:nh + h + 1, :] for h in hs]
    ic = [rep[:, h * LANES:(h + 1) * LANES] for h in hs]
    irow = [gpre_t[h:h + 1, :] for h in hs]
    gtot = [x[lc - 1:lc, :] for x in bc]

    qk_t = [_dot_nt(q[h], kb[h]) for h in hs]
    q_ct = [_dot(q[h], ct_prev[h].astype(BF16)) for h in hs]
    m_loc = [jnp.max(gtot[h] - brow[h] + irow[h], axis=-1, keepdims=True) for h in hs]
    wgt_c = [jnp.exp(gtot[h] - bc[h] + ic[h] - m_loc[h]) for h in hs]
    d_ct = [_dot(k[h].T.astype(BF16), (wgt_c[h] * v[h]).astype(BF16)) for h in hs]

    log_d = [jnp.where(causal, bc[h] - brow[h] + irow[h], NEG_INF) for h in hs]
    m_intra = [jnp.max(log_d[h], axis=-1, keepdims=True) for h in hs]
    qn = [jnp.sum(qf[h] * n_prev[h], axis=-1, keepdims=True) for h in hs]
    log_inter = [bc[h] + m_prev[h] for h in hs]
    m_q = [jnp.maximum(log_inter[h], m_intra[h]) for h in hs]
    s_mat = [qk_t[h] * jnp.exp(log_d[h] - m_q[h]) for h in hs]
    s_sum = [jnp.sum(s_mat[h], axis=-1, keepdims=True) for h in hs]
    s_v = [_dot(s_mat[h].astype(BF16), v[h].astype(BF16)) for h in hs]
    inter = [jnp.exp(log_inter[h] - m_q[h]) for h in hs]
    hh = [(inter[h] * q_ct[h] + s_v[h])
          / jnp.maximum(jnp.abs(s_sum[h] + inter[h] * qn[h]), jnp.exp(-m_q[h])) for h in hs]
    ms = [jnp.mean(hh[h] * hh[h], axis=-1, keepdims=True) for h in hs]
    for h in hs:
        og = _sigmoid(o_all[:, h * dh:(h + 1) * dh])
        out_ref[:, h * dh:(h + 1) * dh] = og * (hh[h] * lax.rsqrt(ms[h] + RMS_EPS)
                                                * gain[:, h * dh:(h + 1) * dh])

    m_next, n_next, ct_next = [], [], []
    for h in hs:
        d_n = jnp.sum(wgt_c[h] * k[h], axis=0, keepdims=True)
        m_new = jnp.maximum(gtot[h] + m_prev[h], m_loc[h])
        a = jnp.exp(gtot[h] + m_prev[h] - m_new)
        bb = jnp.exp(m_loc[h] - m_new)
        ct_next.append(a * ct_prev[h] + bb * d_ct[h])
        n_next.append(a * n_prev[h] + bb * d_n)
        m_next.append(m_new)
    return m_next, n_next, ct_next


def _mlstm(qk, mv, mo, gif, convw, bias, gain, bsz, seq):
    t = bsz * seq
    rows = MLSTM_CHUNKS_PER_STEP * MLSTM_CHUNK
    nc = seq // rows
    row = lambda b, c: (b * nc + c, 0)
    full = lambda b, c: (0, 0)
    r = jnp.arange(MLSTM_CHUNK)
    lower = r[None, :] <= r[:, None]
    tri = jnp.stack([lower, lower.T]).astype(BF16)
    spread = (r[:, None] == jnp.arange(2 * MLSTM_HEADS * LANES)[None, :] // LANES).astype(BF16)
    lc = rows
    return pl.pallas_call(
        _mlstm_kernel,
        grid=(bsz, nc),
        in_specs=[pl.BlockSpec((lc, 2 * MLSTM_WIDTH), row),
                  pl.BlockSpec((lc, MLSTM_WIDTH), row),
                  pl.BlockSpec((lc, MLSTM_WIDTH), row),
                  pl.BlockSpec((lc, LANES), row),
                  pl.BlockSpec((CONV_WIDTH, 2 * MLSTM_WIDTH), full),
                  pl.BlockSpec((1, LANES), full),
                  pl.BlockSpec((1, MLSTM_WIDTH), full),
                  pl.BlockSpec(tri.shape, lambda b, c: (0, 0, 0)),
                  pl.BlockSpec(spread.shape, full)],
        out_specs=pl.BlockSpec((lc, MLSTM_WIDTH), row),
        out_shape=jax.ShapeDtypeStruct((t, MLSTM_WIDTH), F32),
        scratch_shapes=[pltpu.VMEM((lc + SUBLANES, 2 * MLSTM_WIDTH), F32),
                        pltpu.VMEM((MLSTM_HEADS, MLSTM_HEAD_DIM, MLSTM_HEAD_DIM), F32),
                        pltpu.VMEM((SUBLANES, MLSTM_HEAD_DIM), F32),
                        pltpu.VMEM((SUBLANES, LANES), F32)],
        compiler_params=pltpu.CompilerParams(dimension_semantics=("parallel", "arbitrary"),
                                             vmem_limit_bytes=VMEM_LIMIT),
        name="mlstm",
    )(qk, mv, mo, gif, convw, bias, gain, tri, spread)


def _rel_bucket(dist):
    max_exact = REL_BUCKETS // 2
    is_small = dist < max_exact
    large = max_exact + (jnp.log(jnp.maximum(dist, 1).astype(F32) / max_exact)
                         / math.log(REL_MAX_DIST / max_exact) * (REL_BUCKETS - max_exact)).astype(jnp.int32)
    large = jnp.minimum(large, REL_BUCKETS - 1)
    return jnp.where(is_small, dist, large)


LOG2E = math.log2(math.e)
MOBA_LOOKAHEAD = 8


def _moba_bias_tiles(rel_bias):
    bs = MOBA_BLOCK
    d_own = jnp.arange(bs)[None, :] - jnp.arange(bs)[:, None]
    tab = rel_bias.astype(F32) * LOG2E
    buckets = jnp.stack([_rel_bucket(jnp.maximum(d_own, 0)), _rel_bucket(bs + d_own)])
    onehot = (buckets[..., None] == jnp.arange(REL_BUCKETS)).astype(F32)
    tiles = jnp.einsum('tjib,hb->htji', onehot, tab, precision=lax.Precision.HIGHEST)
    causal = jnp.stack([d_own >= 0, jnp.ones_like(d_own, dtype=bool)])
    tiles = jnp.where(causal[None], tiles, NEG_INF)
    far = jnp.take(tab, _rel_bucket(jnp.full((1,), 2 * bs, jnp.int32)), axis=1)
    far_rows = jnp.broadcast_to(far[:, None, :], (MOBA_HEADS, SUBLANES, LANES)).reshape(-1, LANES)
    return tiles, jnp.pad(far_rows, ((0, LANES - MOBA_HEADS * SUBLANES), (0, 0)))


def _moba_kernel(q_ref, k_ref, v_ref, tiles_ref, far_ref, out_ref,
                 kaug_ref, vt_ref, km_ref, qaug_ref, *, nb):
    bs, nh, dh = MOBA_BLOCK, MOBA_HEADS, MOBA_HEAD_DIM
    qi = pl.program_id(1)

    @pl.when(qi == 0)
    def _():
        lane = lax.broadcasted_iota(jnp.int32, (bs, LANES), 1)
        means = []
        for n in range(nb):
            kblk = k_ref[n * bs:(n + 1) * bs, :]
            means.append(jnp.mean(kblk, axis=0, keepdims=True))
            for h in range(nh):
                p, e = divmod(h, 2)
                o = (1 - e) * dh
                hot = jnp.where(lane == o + n, 1.0, jnp.where(lane == o + SUBLANES + n, 1.0, 0.0))
                kaug_ref[h, n] = jnp.where(lane // dh == e, kblk[:, p * LANES:(p + 1) * LANES], hot).astype(BF16)
            vt_ref[n] = v_ref[n * bs:(n + 1) * bs, :].T.astype(BF16)
        means += [jnp.zeros_like(means[0])] * (SUBLANES - nb)
        kmean = jnp.concatenate(means, axis=0)
        rows = lax.broadcasted_iota(jnp.int32, (LANES, MOBA_WIDTH), 0)
        cols = lax.broadcasted_iota(jnp.int32, (LANES, MOBA_WIDTH), 1)
        tiled = jnp.concatenate([kmean] * (LANES // SUBLANES), axis=0)
        km_ref[...] = jnp.where(rows // SUBLANES == cols // dh, tiled, 0.0)

    q = q_ref[...]
    q_hi, q_mid, _ = _split3(q)
    m_hi, m_mid, _ = _split3(km_ref[...])
    gate = _dot_nt(m_hi, q_hi) + _dot_nt(m_mid, q_hi) + _dot_nt(m_hi, q_mid)
    nrows = LANES
    blk = lax.broadcasted_iota(jnp.int32, (nrows, bs), 0) % SUBLANES
    gate = jnp.where(blk < qi, gate, NEG_INF)
    rank = jnp.zeros((nrows, bs), F32)
    for s in range(1, SUBLANES):
        fwd = pltpu.roll(gate, nrows - s, 0)
        back = pltpu.roll(gate, SUBLANES - s, 0)
        wrapped = blk + s >= SUBLANES
        other = jnp.where(wrapped, back, fwd)
        rank = rank + jnp.where(wrapped, jnp.where(other >= gate, 1.0, 0.0), jnp.where(other > gate, 1.0, 0.0))
    chosen = rank < MOBA_TOPK
    far = jnp.concatenate([far_ref[...]] * (bs // LANES), axis=1)
    far_hi = far.astype(BF16).astype(F32)
    far_lo = far - far_hi
    sel_hi = jnp.where(blk < qi - 1, jnp.where(chosen, far_hi, NEG_INF),
                       jnp.where(blk == qi - 1, jnp.where(chosen, 0.0, NEG_INF),
                                 jnp.where(blk == qi, 0.0, NEG_INF)))
    sel_lo = jnp.where(blk < qi - 1, jnp.where(chosen, far_lo, 0.0), 0.0)

    q_t = (q * (dh ** -0.5 * LOG2E)).T
    pad = jnp.zeros((dh - 2 * SUBLANES, bs), F32)
    for h in range(nh):
        e = h % 2
        extra = [sel_hi[h * SUBLANES:(h + 1) * SUBLANES], sel_lo[h * SUBLANES:(h + 1) * SUBLANES], pad]
        q_h = [q_t[h * dh:(h + 1) * dh]]
        qaug_ref[h] = jnp.concatenate(q_h + extra if e == 0 else extra + q_h, axis=0).astype(BF16)

    def scores(h, n):
        return _dot(kaug_ref[h, n], qaug_ref[h])

    def absorb(h, n, s, state):
        mx = jnp.max(s, axis=0, keepdims=True)
        v_t = vt_ref[n, h * dh:(h + 1) * dh, :]
        m_old, l_old, acc = state
        m_new = jnp.maximum(m_old, mx)
        alpha = jnp.exp2(m_old - m_new)
        pr = jnp.exp2(s - m_new)
        return (m_new, alpha * l_old + jnp.sum(pr, axis=0, keepdims=True),
                alpha * acc + _dot(v_t, pr.astype(BF16)))

    def run_tiles(jobs, states):
        states = list(states)
        pending = {}
        for i in range(min(MOBA_LOOKAHEAD, len(jobs))):
            pending[i] = scores(*jobs[i][:2])
        for i, (h, n, tile) in enumerate(jobs):
            s = pending.pop(i)
            if tile is not None:
                s = tile + s
            states[h] = absorb(h, n, s, states[h])
            if i + MOBA_LOOKAHEAD < len(jobs):
                pending[i + MOBA_LOOKAHEAD] = scores(*jobs[i + MOBA_LOOKAHEAD][:2])
        return tuple(states)

    states = tuple((jnp.full((1, bs), NEG_INF, F32), jnp.zeros((1, bs), F32), jnp.zeros((dh, bs), F32))
                   for _ in range(nh))

    def near_body(k, carry):
        return run_tiles([(h, qi - k, tiles_ref[h, k]) for h in range(nh)], carry)

    def far_body(n, carry):
        return run_tiles([(h, n, None) for h in range(nh)], carry)

    states = lax.fori_loop(0, jnp.minimum(qi + 1, 2), near_body, states)
    states = lax.fori_loop(0, jnp.maximum(qi - 1, 0), far_body, states)
    out_t = jnp.concatenate([acc * (1.0 / l) for (_, l, acc) in states], axis=0)
    out_ref[...] = out_t.T


def _moba(aq, ak, av, tiles, far, bsz, seq):
    t = bsz * seq
    bs = MOBA_BLOCK
    nb = seq // bs
    return pl.pallas_call(
        functools.partial(_moba_kernel, nb=nb),
        grid=(bsz, nb),
        in_specs=[pl.BlockSpec((bs, MOBA_WIDTH), lambda b, i: (b * nb + i, 0)),
                  pl.BlockSpec((seq, MOBA_WIDTH), lambda b, i: (b, 0)),
                  pl.BlockSpec((seq, MOBA_WIDTH), lambda b, i: (b, 0)),
                  pl.BlockSpec(tiles.shape, lambda b, i: (0, 0, 0, 0)),
                  pl.BlockSpec(far.shape, lambda b, i: (0, 0))],
        out_specs=pl.BlockSpec((bs, MOBA_WIDTH), lambda b, i: (b * nb + i, 0)),
        out_shape=jax.ShapeDtypeStruct((t, MOBA_WIDTH), F32),
        scratch_shapes=[pltpu.VMEM((MOBA_HEADS, nb, bs, LANES), BF16),
                        pltpu.VMEM((nb, MOBA_WIDTH, bs), BF16),
                        pltpu.VMEM((LANES, MOBA_WIDTH), F32),
                        pltpu.VMEM((MOBA_HEADS, LANES, bs), BF16)],
        compiler_params=pltpu.CompilerParams(dimension_semantics=("parallel", "arbitrary"),
                                             vmem_limit_bytes=VMEM_LIMIT),
        name="moba",
    )(aq, ak, av, tiles, far)


def _merge_kernel(x_ref, ys_ref, ym_ref, ya_ref, gate_ref, wps_ref, wpm_ref, wpa_ref, wo_ref, g_ref, o_ref):
    def gate(i):
        return gate_ref[:, i * D_MODEL:(i + 1) * D_MODEL].astype(F32)

    merged = (gate(0) * _dot(ys_ref[...].astype(BF16), wps_ref[...])
              + gate(1) * _dot(ym_ref[...].astype(BF16), wpm_ref[...])
              + gate(2) * _dot(ya_ref[...].astype(BF16), wpa_ref[...]))
    z = _dot(merged.astype(BF16), wo_ref[...])
    o_ref[...] = x_ref[...] + _rms(z, g_ref[...])


def _merge(x2d, ys, ym, ya, gate, wps, wpm, wpa, wo, g, ts):
    t = x2d.shape[0]
    row = lambda i: (i, 0)
    full = lambda i: (0, 0)
    return pl.pallas_call(
        _merge_kernel,
        grid=(t // ts,),
        in_specs=[pl.BlockSpec((ts, D_MODEL), row),
                  pl.BlockSpec((ts, SSM_WIDTH), row),
                  pl.BlockSpec((ts, MLSTM_WIDTH), row),
                  pl.BlockSpec((ts, MOBA_WIDTH), row),
                  pl.BlockSpec((ts, 3 * D_MODEL), row),
                  pl.BlockSpec(wps.shape, full),
                  pl.BlockSpec(wpm.shape, full),
                  pl.BlockSpec(wpa.shape, full),
                  pl.BlockSpec(wo.shape, full),
                  pl.BlockSpec((1, D_MODEL), full)],
        out_specs=pl.BlockSpec((ts, D_MODEL), row),
        out_shape=jax.ShapeDtypeStruct((t, D_MODEL), F32),
        compiler_params=pltpu.CompilerParams(dimension_semantics=("parallel",),
                                             vmem_limit_bytes=VMEM_LIMIT),
        name="merge",
    )(x2d, ys, ym, ya, gate, wps, wpm, wpa, wo, g)


def _ffn_kernel(x_ref, g1_ref, w1_ref, w2_ref, g2_ref, o_ref):
    x = x_ref[...]
    hb = _rms(x, g1_ref[...]).astype(BF16)
    a = jnp.maximum(_dot(hb, w1_ref[...]), 0.0)
    f = _dot((a * a).astype(BF16), w2_ref[...])
    o_ref[...] = x + _rms(f, g2_ref[...])


def _ffn(x2d, g1, w1, w2, g2, ts):
    t = x2d.shape[0]
    full = lambda i: (0, 0)
    return pl.pallas_call(
        _ffn_kernel,
        grid=(t // ts,),
        in_specs=[pl.BlockSpec((ts, D_MODEL), lambda i: (i, 0)),
                  pl.BlockSpec((1, D_MODEL), full),
                  pl.BlockSpec((D_MODEL, D_FF), full, pipeline_mode=pl.Buffered(1)),
                  pl.BlockSpec((D_FF, D_MODEL), full, pipeline_mode=pl.Buffered(1)),
                  pl.BlockSpec((1, D_MODEL), full)],
        out_specs=pl.BlockSpec((ts, D_MODEL), lambda i: (i, 0)),
        out_shape=jax.ShapeDtypeStruct((t, D_MODEL), F32),
        compiler_params=pltpu.CompilerParams(dimension_semantics=("parallel",),
                                             vmem_limit_bytes=VMEM_LIMIT),
        name="ffn",
    )(x2d, g1, w1, w2, g2)


def _split_w_in(w_in):
    n_if = 2 * MLSTM_HEADS
    c_if = SSM_WIDTH + 4 * MLSTM_WIDTH
    assert c_if == C_AQ and w_in.shape[-1] - n_if == C_GIF
    wa = w_in[..., :c_if].astype(BF16)
    wb = w_in[..., c_if + n_if:].astype(BF16)
    wg = jnp.pad(w_in[..., c_if:c_if + n_if], ((0, 0), (0, 0), (0, LANES - n_if))).astype(BF16)
    return wa, wb, wg


def kernel(x, w_in, conv_w, ssm_a_re, ssm_a_im, ssm_log_dt, ssm_b_re, ssm_b_im, ssm_c_re, ssm_c_im, ssm_d, ssm_w_glu, mlstm_i_bias, mlstm_f_bias, mlstm_head_gain, rel_bias, w_ssm_proj, w_mlstm_proj, w_moba_proj, w_out, w_ff1, w_ff2, norm_mix_pre, norm_mix_post, norm_ffn_pre, norm_ffn_post):
    bsz, seq, _ = x.shape
    depth = w_in.shape[0]
    assert bsz == SUBLANES and seq % MOBA_BLOCK == 0 and seq // MOBA_BLOCK <= SUBLANES
    ts_in = 256
    ts = 512
    tc = 64
    x2d = x.reshape(bsz * seq, D_MODEL)
    wa, wb, wg = _split_w_in(w_in)
    wps, wpm, wpa = w_ssm_proj.astype(BF16), w_mlstm_proj.astype(BF16), w_moba_proj.astype(BF16)
    wo = w_out.astype(BF16)
    w1 = w_ff1.astype(BF16)
    w2 = w_ff2.astype(BF16)
    wglu = ssm_w_glu.astype(BF16)
    gate_bias = jnp.pad(jnp.concatenate([mlstm_i_bias, mlstm_f_bias], axis=-1).astype(F32),
                        ((0, 0), (0, LANES - 2 * MLSTM_HEADS)))
    tiles, far = _moba_bias_tiles(rel_bias)
    for l in range(depth):
        u, qk, mv, mo, aq, ak, av, gate, gif = _inproj(x2d, norm_mix_pre[l][None, :], wa[l], wb[l], wg[l],
                                                       ts_in)
        bblk, cblk, ar_b, ai_b = _ssm_params(ssm_a_re[l], ssm_a_im[l], ssm_log_dt[l], ssm_b_re[l], ssm_b_im[l],
                                             ssm_c_re[l], ssm_c_im[l], bsz)
        y_ssm = _ssm(u.reshape(bsz, seq, SSM_WIDTH), bblk, cblk, ar_b, ai_b, ssm_d[l][None, :], wglu[l],
                     bsz, seq, tc)
        y_mlstm = _mlstm(qk, mv, mo, gif, conv_w[l].astype(F32), gate_bias[l][None, :],
                         mlstm_head_gain[l][None, :].astype(F32), bsz, seq)
        y_moba = _moba(aq, ak, av, tiles, far, bsz, seq)
        x2d = _merge(x2d, y_ssm.reshape(bsz * seq, SSM_WIDTH), y_mlstm, y_moba, gate, wps[l], wpm[l], wpa[l],
                     wo[l], norm_mix_post[l][None, :], ts)
        x2d = _ffn(x2d, norm_ffn_pre[l][None, :], w1[l], w2[l], norm_ffn_post[l][None, :], ts)
    return x2d.reshape(bsz, seq, D_MODEL)
```

```python
import functools
import math

import jax
import jax.numpy as jnp
from jax import lax
from jax.experimental import pallas as pl
from jax.experimental.pallas import tpu as pltpu

F32 = jnp.float32
BF16 = jnp.bfloat16

D_MODEL = 1024
SSM_WIDTH = 512
SSM_GROUP = 16
SSM_GROUPS = 32
SSM_STATE = 64
MLSTM_WIDTH = 512
MLSTM_HEADS = 4
MLSTM_HEAD_DIM = 128
MLSTM_CHUNK = 128
MLSTM_CHUNKS_PER_STEP = 4
CONV_WIDTH = 4
MOBA_WIDTH = 512
MOBA_HEADS = 8
MOBA_HEAD_DIM = 64
MOBA_BLOCK = 256
MOBA_TOPK = 3
REL_BUCKETS = 32
REL_MAX_DIST = 128
D_FF = 4 * D_MODEL
RMS_EPS = 1e-6
NEG_INF = -1e30

LANES = 128
SUBLANES = 8
VMEM_LIMIT = 56 * 1024 * 1024

C_U = 0
C_QK = 512
C_MV = 1536
C_MO = 2048
C_AQ = 2560
C_AK = 3072
C_AV = 3584
C_GATE = 4096
C_GIF = 7168
C_END = 7296


def _rms(x, g):
    return x * lax.rsqrt(jnp.mean(x * x, axis=-1, keepdims=True) + RMS_EPS) * g


def _sigmoid(x):
    return 1.0 / (1.0 + jnp.exp(-x))


def _dot(a, b):
    return jnp.dot(a, b, preferred_element_type=F32)


def _dot_nt(a, b):
    return lax.dot_general(a, b, (((1,), (1,)), ((), ())), preferred_element_type=F32)


def _split3(a):
    hi = a.astype(BF16)
    r1 = a - hi.astype(F32)
    mid = r1.astype(BF16)
    lo = (r1 - mid.astype(F32)).astype(BF16)
    return hi, mid, lo


def _inproj_kernel(x_ref, g_ref, wa_ref, wb_ref, wg_ref, u_ref, qk_ref, mv_ref, mo_ref, aq_ref, ak_ref, av_ref,
                   gate_ref, gif_ref):
    hb = _rms(x_ref[...], g_ref[...]).astype(BF16)

    def proj(w_ref, a, b):
        return _dot(hb, w_ref[:, a:b])

    u_ref[...] = proj(wa_ref, C_U, C_QK)
    qk_ref[...] = proj(wa_ref, C_QK, C_MV)
    mv_ref[...] = proj(wa_ref, C_MV, C_MO)
    mo_ref[...] = proj(wa_ref, C_MO, C_AQ)
    aq_ref[...] = proj(wb_ref, C_AQ - C_AQ, C_AK - C_AQ)
    ak_ref[...] = proj(wb_ref, C_AK - C_AQ, C_AV - C_AQ)
    av_ref[...] = proj(wb_ref, C_AV - C_AQ, C_GATE - C_AQ)
    gate_ref[...] = _sigmoid(proj(wb_ref, C_GATE - C_AQ, C_GIF - C_AQ)).astype(BF16)
    gif_ref[...] = _dot(hb, wg_ref[...])


def _inproj(x2d, g, wa, wb, wg, ts):
    t = x2d.shape[0]
    row = lambda i: (i, 0)
    full = lambda i: (0, 0)

    def out(n, dtype=F32):
        return jax.ShapeDtypeStruct((t, n), dtype), pl.BlockSpec((ts, n), row)

    outs = [out(512), out(1024), out(512), out(512), out(512), out(512), out(512), out(3 * D_MODEL, BF16),
            out(LANES)]
    return pl.pallas_call(
        _inproj_kernel,
        grid=(t // ts,),
        in_specs=[pl.BlockSpec((ts, D_MODEL), row),
                  pl.BlockSpec((1, D_MODEL), full),
                  pl.BlockSpec(wa.shape, full, pipeline_mode=pl.Buffered(1)),
                  pl.BlockSpec(wb.shape, full, pipeline_mode=pl.Buffered(1)),
                  pl.BlockSpec(wg.shape, full, pipeline_mode=pl.Buffered(1))],
        out_specs=[o[1] for o in outs],
        out_shape=[o[0] for o in outs],
        compiler_params=pltpu.CompilerParams(dimension_semantics=("parallel",),
                                             vmem_limit_bytes=VMEM_LIMIT),
        name="inproj",
    )(x2d, g, wa, wb, wg)


SSM_KB = 2
SSM_KB_CH = SSM_WIDTH // SSM_KB
SSM_KB_ST = SSM_KB_CH // SSM_GROUP * SSM_STATE
SSM_SLAB = 1024


def _gelu(x):
    return 0.5 * x * (1.0 + jnp.tanh(math.sqrt(2.0 / math.pi) * (x + 0.044715 * (x * x * x))))


def _ssm_kernel(u_ref, perm_ref, perm_t_ref, bblk_ref, cblk_ref, ar_ref, ai_ref, d_ref, wglu_ref, o_ref,
                xs_ref, st_ref, *, bsz, tc):
    @pl.when(pl.program_id(0) == 0)
    def _():
        st_ref[...] = jnp.zeros_like(st_ref)

    u_bm = u_ref[...].reshape(bsz * tc, SSM_WIDTH)
    u_hi = u_bm.astype(BF16)
    u_lo = (u_bm - u_hi.astype(F32)).astype(BF16)
    u_tm_hi = _dot(perm_ref[...], u_hi)
    u = u_tm_hi + _dot(perm_ref[...], u_lo)
    ub = u_tm_hi.astype(BF16)
    for kb in range(SSM_KB):
        xs_ref[:, kb * 2 * SSM_KB_ST:(kb + 1) * 2 * SSM_KB_ST] = _dot(
            ub[:, kb * SSM_KB_CH:(kb + 1) * SSM_KB_CH], bblk_ref[kb])

    for kb in range(SSM_KB):
        for j in range(SSM_KB_ST // SSM_SLAB):
            re0 = kb * 2 * SSM_KB_ST + j * SSM_SLAB
            im0 = re0 + SSM_KB_ST
            co = kb * SSM_KB_ST + j * SSM_SLAB

            def body(t, carry, re0=re0, im0=im0, co=co):
                xr, xi = carry
                ar = ar_ref[:, co:co + SSM_SLAB]
                ai = ai_ref[:, co:co + SSM_SLAB]
                r0 = pl.multiple_of(t * bsz, bsz)
                br = xs_ref[pl.ds(r0, bsz), re0:re0 + SSM_SLAB]
                bi = xs_ref[pl.ds(r0, bsz), im0:im0 + SSM_SLAB]
                nr = ar * xr - ai * xi + br
                ni = ar * xi + ai * xr + bi
                xs_ref[pl.ds(r0, bsz), re0:re0 + SSM_SLAB] = nr
                xs_ref[pl.ds(r0, bsz), im0:im0 + SSM_SLAB] = ni
                return nr, ni

            xr, xi = lax.fori_loop(0, tc, body,
                                   (st_ref[:, re0:re0 + SSM_SLAB], st_ref[:, im0:im0 + SSM_SLAB]),
                                   unroll=8)
            st_ref[:, re0:re0 + SSM_SLAB] = xr
            st_ref[:, im0:im0 + SSM_SLAB] = xi

    ys = []
    for kb in range(SSM_KB):
        xb = xs_ref[:, kb * 2 * SSM_KB_ST:(kb + 1) * 2 * SSM_KB_ST].astype(BF16)
        ys.append(_dot(xb, cblk_ref[kb]))
    y = jnp.concatenate(ys, axis=-1) + d_ref[...] * u
    yg = _gelu(y)
    out_tm = (yg * _sigmoid(_dot(yg.astype(BF16), wglu_ref[...]))).astype(BF16)
    o_ref[...] = _dot(perm_t_ref[...], out_tm).astype(BF16).reshape(bsz, tc, SSM_WIDTH)


def _ssm(u3, bblk, cblk, ar, ai, d, wglu, bsz, seq, tc):
    rows = tc * bsz
    full2 = lambda c: (0, 0)
    full3 = lambda c: (0, 0, 0)
    r = jnp.arange(rows)
    perm = (r[None, :] == (r[:, None] % bsz) * tc + r[:, None] // bsz).astype(BF16)
    return pl.pallas_call(
        functools.partial(_ssm_kernel, bsz=bsz, tc=tc),
        grid=(seq // tc,),
        in_specs=[pl.BlockSpec((bsz, tc, SSM_WIDTH), lambda c: (0, c, 0)),
                  pl.BlockSpec((rows, rows), full2),
                  pl.BlockSpec((rows, rows), full2),
                  pl.BlockSpec(bblk.shape, full3),
                  pl.BlockSpec(cblk.shape, full3),
                  pl.BlockSpec(ar.shape, full2),
                  pl.BlockSpec(ai.shape, full2),
                  pl.BlockSpec((1, SSM_WIDTH), full2),
                  pl.BlockSpec((SSM_WIDTH, SSM_WIDTH), full2)],
        out_specs=pl.BlockSpec((bsz, tc, SSM_WIDTH), lambda c: (0, c, 0)),
        out_shape=jax.ShapeDtypeStruct((bsz, seq, SSM_WIDTH), BF16),
        scratch_shapes=[pltpu.VMEM((rows, SSM_KB * 2 * SSM_KB_ST), F32),
                        pltpu.VMEM((bsz, SSM_KB * 2 * SSM_KB_ST), F32)],
        compiler_params=pltpu.CompilerParams(dimension_semantics=("arbitrary",),
                                             vmem_limit_bytes=VMEM_LIMIT),
        name="ssm",
    )(u3, perm, perm.T, bblk, cblk, ar, ai, d, wglu)


def _ssm_params(a_re, a_im, log_dt, b_re, b_im, c_re, c_im, bsz):
    ar, ai = a_re.astype(F32), a_im.astype(F32)
    dt = jnp.exp(log_dt.astype(F32))[:, None]
    decay = jnp.exp(dt * ar)
    abar_r, abar_i = decay * jnp.cos(dt * ai), decay * jnp.sin(dt * ai)
    den = ar * ar + ai * ai
    nr, ni = abar_r - 1.0, abar_i
    fr, fi = (nr * ar + ni * ai) / den, (ni * ar - nr * ai) / den
    br, bi = b_re.astype(F32), b_im.astype(F32)
    bbar_r = fr[..., None] * br - fi[..., None] * bi
    bbar_i = fr[..., None] * bi + fi[..., None] * br
    gl = SSM_GROUPS // SSM_KB
    eye = jnp.eye(gl, dtype=F32)

    def pack_b(bb):
        bb = bb.reshape(SSM_KB, gl, SSM_STATE, SSM_GROUP)
        dense = jnp.einsum('kgph,gq->kghqp', bb, eye)
        return dense.reshape(SSM_KB, gl * SSM_GROUP, gl * SSM_STATE)

    def pack_c(cc):
        cc = cc.reshape(SSM_KB, gl, SSM_GROUP, SSM_STATE)
        dense = jnp.einsum('kghp,gq->kgpqh', cc, eye)
        return dense.reshape(SSM_KB, gl * SSM_STATE, gl * SSM_GROUP)

    bblk = jnp.concatenate([pack_b(bbar_r), pack_b(bbar_i)], axis=-1).astype(BF16)
    cblk = jnp.concatenate([pack_c(c_re.astype(F32)), pack_c(-c_im.astype(F32))], axis=1).astype(BF16)
    ar_b = jnp.broadcast_to(abar_r.reshape(1, -1), (bsz, SSM_GROUPS * SSM_STATE))
    ai_b = jnp.broadcast_to(abar_i.reshape(1, -1), (bsz, SSM_GROUPS * SSM_STATE))
    return bblk, cblk, ar_b, ai_b


def _mlstm_kernel(qk_ref, v_ref, o_ref, gif_ref, convw_ref, bias_ref, gain_ref, tri_ref, spread_ref, out_ref,
                  xp_ref, ct_ref, n_ref, m_ref):
    lc, dh, nh = MLSTM_CHUNK, MLSTM_HEAD_DIM, MLSTM_HEADS
    halo = SUBLANES

    @pl.when(pl.program_id(1) == 0)
    def _():
        xp_ref[0:halo, :] = jnp.zeros((halo, 2 * MLSTM_WIDTH), F32)
        ct_ref[...] = jnp.zeros_like(ct_ref)
        n_ref[...] = jnp.zeros_like(n_ref)
        m_ref[...] = jnp.zeros_like(m_ref)

    rows = qk_ref.shape[0]
    xp_ref[halo:halo + rows, :] = qk_ref[...]
    conv = convw_ref[CONV_WIDTH - 1:CONV_WIDTH, :] * xp_ref[halo:halo + rows, :]
    for j in range(CONV_WIDTH - 1):
        off = halo - (CONV_WIDTH - 1) + j
        conv = conv + convw_ref[j:j + 1, :] * xp_ref[off:off + rows, :]
    xp_ref[0:halo, :] = xp_ref[rows:rows + halo, :]
    qk_all = conv * _sigmoid(conv)

    hs = range(nh)
    state = ([m_ref[h:h + 1, :] for h in hs],
             [n_ref[h:h + 1, :] for h in hs],
             [ct_ref[h] for h in hs])
    for c in range(rows // lc):
        r0 = c * lc
        state = _mlstm_chunk(qk_all[r0:r0 + lc], v_ref[r0:r0 + lc, :], o_ref[r0:r0 + lc, :],
                             gif_ref[r0:r0 + lc, :] + bias_ref[...], gain_ref[...], tri_ref, spread_ref[...],
                             state, out_ref.at[r0:r0 + lc, :])
    for h in hs:
        m_ref[h:h + 1, :] = state[0][h]
        n_ref[h:h + 1, :] = state[1][h]
        ct_ref[h] = state[2][h]


def _mlstm_chunk(qk, v_all, o_all, gpre, gain, tri_ref, spread, state, out_ref):
    lc, dh, nh = MLSTM_CHUNK, MLSTM_HEAD_DIM, MLSTM_HEADS
    m_prev, n_prev, ct_prev = state
    lsig = jnp.minimum(gpre, 0.0) - jnp.log(1.0 + jnp.exp(-jnp.abs(gpre)))
    row_i = lax.broadcasted_iota(jnp.int32, (lc, lc), 0)
    col_i = lax.broadcasted_iota(jnp.int32, (lc, lc), 1)
    causal = col_i <= row_i
    tri = tri_ref[0]
    tri_t = tri_ref[1]
    l_hi, l_mid, l_lo = _split3(lsig)
    bcum_c = _dot(tri, l_hi) + _dot(tri, l_mid) + _dot(tri, l_lo)
    lsig_t = lsig.T
    t_hi, t_mid, t_lo = _split3(lsig_t)
    bcum_r = _dot(t_hi, tri_t) + _dot(t_mid, tri_t) + _dot(t_lo, tri_t)
    gpre_t = gpre.T
    per_time = jnp.where(col_i < nh, gpre, bcum_c)
    p_hi, p_mid, p_lo = _split3(per_time)
    rep = _dot(p_hi, spread) + _dot(p_mid, spread) + _dot(p_lo, spread)

    hs = range(nh)
    qf = [qk[:, h * dh:(h + 1) * dh] for h in hs]
    q = [x.astype(BF16) for x in qf]
    k = [qk[:, MLSTM_WIDTH + h * dh:MLSTM_WIDTH + (h + 1) * dh] * (dh ** -0.5) for h in hs]
    kb = [x.astype(BF16) for x in k]
    v = [v_all[:, h * dh:(h + 1) * dh] for h in hs]
    bc = [rep[:, (nh + h) * LANES:(nh + h + 1) * LANES] for h in hs]
    brow = [bcum_r[nh + h:nh + h + 1, :] for h in hs]
    ic = [rep[:, h * LANES:(h + 1) * LANES] for h in hs]
    irow = [gpre_t[h:h + 1, :] for h in hs]
    gtot = [x[lc - 1:lc, :] for x in bc]

    qk_t = [_dot_nt(q[h], kb[h]) for h in hs]
    q_ct = [_dot(q[h], ct_prev[h].astype(BF16)) for h in hs]
    m_loc = [jnp.max(gtot[h] - brow[h] + irow[h], axis=-1, keepdims=True) for h in hs]
    wgt_c = [jnp.exp(gtot[h] - bc[h] + ic[h] - m_loc[h]) for h in hs]
    d_ct = [_dot(k[h].T.astype(BF16), (wgt_c[h] * v[h]).astype(BF16)) for h in hs]

    log_d = [jnp.where(causal, bc[h] - brow[h] + irow[h], NEG_INF) for h in hs]
    m_intra = [jnp.max(log_d[h], axis=-1, keepdims=True) for h in hs]
    qn = [jnp.sum(qf[h] * n_prev[h], axis=-1, keepdims=True) for h in hs]
    log_inter = [bc[h] + m_prev[h] for h in hs]
    m_q = [jnp.maximum(log_inter[h], m_intra[h]) for h in hs]
    s_mat = [qk_t[h] * jnp.exp(log_d[h] - m_q[h]) for h in hs]
    s_sum = [jnp.sum(s_mat[h], axis=-1, keepdims=True) for h in hs]
    s_v = [_dot(s_mat[h].astype(BF16), v[h].astype(BF16)) for h in hs]
    inter = [jnp.exp(log_inter[h] - m_q[h]) for h in hs]
    hh = [(inter[h] * q_ct[h] + s_v[h])
          / jnp.maximum(jnp.abs(s_sum[h] + inter[h] * qn[h]), jnp.exp(-m_q[h])) for h in hs]
    ms = [jnp.mean(hh[h] * hh[h], axis=-1, keepdims=True) for h in hs]
    for h in hs:
        og = _sigmoid(o_all[:, h * dh:(h + 1) * dh])
        out_ref[:, h * dh:(h + 1) * dh] = og * (hh[h] * lax.rsqrt(ms[h] + RMS_EPS)
                                                * gain[:, h * dh:(h + 1) * dh])

    m_next, n_next, ct_next = [], [], []
    for h in hs:
        d_n = jnp.sum(wgt_c[h] * k[h], axis=0, keepdims=True)
        m_new = jnp.maximum(gtot[h] + m_prev[h], m_loc[h])
        a = jnp.exp(gtot[h] + m_prev[h] - m_new)
        bb = jnp.exp(m_loc[h] - m_new)
        ct_next.append(a * ct_prev[h] + bb * d_ct[h])
        n_next.append(a * n_prev[h] + bb * d_n)
        m_next.append(m_new)
    return m_next, n_next, ct_next


def _mlstm(qk, mv, mo, gif, convw, bias, gain, bsz, seq):
    t = bsz * seq
    rows = MLSTM_CHUNKS_PER_STEP * MLSTM_CHUNK
    nc = seq // rows
    row = lambda b, c: (b * nc + c, 0)
    full = lambda b, c: (0, 0)
    r = jnp.arange(MLSTM_CHUNK)
    lower = r[None, :] <= r[:, None]
    tri = jnp.stack([lower, lower.T]).astype(BF16)
    spread = (r[:, None] == jnp.arange(2 * MLSTM_HEADS * LANES)[None, :] // LANES).astype(BF16)
    lc = rows
    return pl.pallas_call(
        _mlstm_kernel,
        grid=(bsz, nc),
        in_specs=[pl.BlockSpec((lc, 2 * MLSTM_WIDTH), row),
                  pl.BlockSpec((lc, MLSTM_WIDTH), row),
                  pl.BlockSpec((lc, MLSTM_WIDTH), row),
                  pl.BlockSpec((lc, LANES), row),
                  pl.BlockSpec((CONV_WIDTH, 2 * MLSTM_WIDTH), full),
                  pl.BlockSpec((1, LANES), full),
                  pl.BlockSpec((1, MLSTM_WIDTH), full),
                  pl.BlockSpec(tri.shape, lambda b, c: (0, 0, 0)),
                  pl.BlockSpec(spread.shape, full)],
        out_specs=pl.BlockSpec((lc, MLSTM_WIDTH), row),
        out_shape=jax.ShapeDtypeStruct((t, MLSTM_WIDTH), F32),
        scratch_shapes=[pltpu.VMEM((lc + SUBLANES, 2 * MLSTM_WIDTH), F32),
                        pltpu.VMEM((MLSTM_HEADS, MLSTM_HEAD_DIM, MLSTM_HEAD_DIM), F32),
                        pltpu.VMEM((SUBLANES, MLSTM_HEAD_DIM), F32),
                        pltpu.VMEM((SUBLANES, LANES), F32)],
        compiler_params=pltpu.CompilerParams(dimension_semantics=("parallel", "arbitrary"),
                                             vmem_limit_bytes=VMEM_LIMIT),
        name="mlstm",
    )(qk, mv, mo, gif, convw, bias, gain, tri, spread)


def _rel_bucket(dist):
    max_exact = REL_BUCKETS // 2
    is_small = dist < max_exact
    large = max_exact + (jnp.log(jnp.maximum(dist, 1).astype(F32) / max_exact)
                         / math.log(REL_MAX_DIST / max_exact) * (REL_BUCKETS - max_exact)).astype(jnp.int32)
    large = jnp.minimum(large, REL_BUCKETS - 1)
    return jnp.where(is_small, dist, large)


LOG2E = math.log2(math.e)
MOBA_LOOKAHEAD = 8


def _moba_bias_tiles(rel_bias):
    bs = MOBA_BLOCK
    d_own = jnp.arange(bs)[None, :] - jnp.arange(bs)[:, None]
    tab = rel_bias.astype(F32) * LOG2E
    buckets = jnp.stack([_rel_bucket(jnp.maximum(d_own, 0)), _rel_bucket(bs + d_own)])
    onehot = (buckets[..., None] == jnp.arange(REL_BUCKETS)).astype(F32)
    tiles = jnp.einsum('tjib,hb->htji', onehot, tab, precision=lax.Precision.HIGHEST)
    causal = jnp.stack([d_own >= 0, jnp.ones_like(d_own, dtype=bool)])
    tiles = jnp.where(causal[None], tiles, NEG_INF)
    far = jnp.take(tab, _rel_bucket(jnp.full((1,), 2 * bs, jnp.int32)), axis=1)
    far_rows = jnp.broadcast_to(far[:, None, :], (MOBA_HEADS, SUBLANES, LANES)).reshape(-1, LANES)
    return tiles, jnp.pad(far_rows, ((0, LANES - MOBA_HEADS * SUBLANES), (0, 0)))


def _moba_kernel(q_ref, k_ref, v_ref, tiles_ref, far_ref, out_ref,
                 kaug_ref, vt_ref, km_ref, qaug_ref, *, nb):
    bs, nh, dh = MOBA_BLOCK, MOBA_HEADS, MOBA_HEAD_DIM
    qi = pl.program_id(1)

    @pl.when(qi == 0)
    def _():
        lane = lax.broadcasted_iota(jnp.int32, (bs, LANES), 1)
        means = []
        for n in range(nb):
            kblk = k_ref[n * bs:(n + 1) * bs, :]
            means.append(jnp.mean(kblk, axis=0, keepdims=True))
            for h in range(nh):
                p, e = divmod(h, 2)
                o = (1 - e) * dh
                hot = jnp.where(lane == o + n, 1.0, jnp.where(lane == o + SUBLANES + n, 1.0, 0.0))
                kaug_ref[h, n] = jnp.where(lane // dh == e, kblk[:, p * LANES:(p + 1) * LANES], hot).astype(BF16)
            vt_ref[n] = v_ref[n * bs:(n + 1) * bs, :].T.astype(BF16)
        means += [jnp.zeros_like(means[0])] * (SUBLANES - nb)
        kmean = jnp.concatenate(means, axis=0)
        rows = lax.broadcasted_iota(jnp.int32, (LANES, MOBA_WIDTH), 0)
        cols = lax.broadcasted_iota(jnp.int32, (LANES, MOBA_WIDTH), 1)
        tiled = jnp.concatenate([kmean] * (LANES // SUBLANES), axis=0)
        km_ref[...] = jnp.where(rows // SUBLANES == cols // dh, tiled, 0.0)

    q = q_ref[...]
    q_hi, q_mid, _ = _split3(q)
    m_hi, m_mid, _ = _split3(km_ref[...])
    gate = _dot_nt(m_hi, q_hi) + _dot_nt(m_mid, q_hi) + _dot_nt(m_hi, q_mid)
    nrows = LANES
    blk = lax.broadcasted_iota(jnp.int32, (nrows, bs), 0) % SUBLANES
    gate = jnp.where(blk < qi, gate, NEG_INF)
    rank = jnp.zeros((nrows, bs), F32)
    for s in range(1, SUBLANES):
        fwd = pltpu.roll(gate, nrows - s, 0)
        back = pltpu.roll(gate, SUBLANES - s, 0)
        wrapped = blk + s >= SUBLANES
        other = jnp.where(wrapped, back, fwd)
        rank = rank + jnp.where(wrapped, jnp.where(other >= gate, 1.0, 0.0), jnp.where(other > gate, 1.0, 0.0))
    chosen = rank < MOBA_TOPK
    far = jnp.concatenate([far_ref[...]] * (bs // LANES), axis=1)
    far_hi = far.astype(BF16).astype(F32)
    far_lo = far - far_hi
    sel_hi = jnp.where(blk < qi - 1, jnp.where(chosen, far_hi, NEG_INF),
                       jnp.where(blk == qi - 1, jnp.where(chosen, 0.0, NEG_INF),
                                 jnp.where(blk == qi, 0.0, NEG_INF)))
    sel_lo = jnp.where(blk < qi - 1, jnp.where(chosen, far_lo, 0.0), 0.0)

    q_t = (q * (dh ** -0.5 * LOG2E)).T
    pad = jnp.zeros((dh - 2 * SUBLANES, bs), F32)
    for h in range(nh):
        e = h % 2
        extra = [sel_hi[h * SUBLANES:(h + 1) * SUBLANES], sel_lo[h * SUBLANES:(h + 1) * SUBLANES], pad]
        q_h = [q_t[h * dh:(h + 1) * dh]]
        qaug_ref[h] = jnp.concatenate(q_h + extra if e == 0 else extra + q_h, axis=0).astype(BF16)

    def scores(h, n):
        return _dot(kaug_ref[h, n], qaug_ref[h])

    def absorb(h, n, s, state):
        mx = jnp.max(s, axis=0, keepdims=True)
        v_t = vt_ref[n, h * dh:(h + 1) * dh, :]
        m_old, l_old, acc = state
        m_new = jnp.maximum(m_old, mx)
        alpha = jnp.exp2(m_old - m_new)
        pr = jnp.exp2(s - m_new)
        return (m_new, alpha * l_old + jnp.sum(pr, axis=0, keepdims=True),
                alpha * acc + _dot(v_t, pr.astype(BF16)))

    def run_tiles(jobs, states):
        states = list(states)
        pending = {}
        for i in range(min(MOBA_LOOKAHEAD, len(jobs))):
            pending[i] = scores(*jobs[i][:2])
        for i, (h, n, tile) in enumerate(jobs):
            s = pending.pop(i)
            if tile is not None:
                s = tile + s
            states[h] = absorb(h, n, s, states[h])
            if i + MOBA_LOOKAHEAD < len(jobs):
                pending[i + MOBA_LOOKAHEAD] = scores(*jobs[i + MOBA_LOOKAHEAD][:2])
        return tuple(states)

    states = tuple((jnp.full((1, bs), NEG_INF, F32), jnp.zeros((1, bs), F32), jnp.zeros((dh, bs), F32))
                   for _ in range(nh))

    def near_body(k, carry):
        return run_tiles([(h, qi - k, tiles_ref[h, k]) for h in range(nh)], carry)

    def far_body(n, carry):
        return run_tiles([(h, n, None) for h in range(nh)], carry)

    states = lax.fori_loop(0, jnp.minimum(qi + 1, 2), near_body, states)
    states = lax.fori_loop(0, jnp.maximum(qi - 1, 0), far_body, states)
    out_t = jnp.concatenate([acc * (1.0 / l) for (_, l, acc) in states], axis=0)
    out_ref[...] = out_t.T


def _moba(aq, ak, av, tiles, far, bsz, seq):
    t = bsz * seq
    bs = MOBA_BLOCK
    nb = seq // bs
    return pl.pallas_call(
        functools.partial(_moba_kernel, nb=nb),
        grid=(bsz, nb),
        in_specs=[pl.BlockSpec((bs, MOBA_WIDTH), lambda b, i: (b * nb + i, 0)),
                  pl.BlockSpec((seq, MOBA_WIDTH), lambda b, i: (b, 0)),
                  pl.BlockSpec((seq, MOBA_WIDTH), lambda b, i: (b, 0)),
                  pl.BlockSpec(tiles.shape, lambda b, i: (0, 0, 0, 0)),
                  pl.BlockSpec(far.shape, lambda b, i: (0, 0))],
        out_specs=pl.BlockSpec((bs, MOBA_WIDTH), lambda b, i: (b * nb + i, 0)),
        out_shape=jax.ShapeDtypeStruct((t, MOBA_WIDTH), F32),
        scratch_shapes=[pltpu.VMEM((MOBA_HEADS, nb, bs, LANES), BF16),
                        pltpu.VMEM((nb, MOBA_WIDTH, bs), BF16),
                        pltpu.VMEM((LANES, MOBA_WIDTH), F32),
                        pltpu.VMEM((MOBA_HEADS, LANES, bs), BF16)],
        compiler_params=pltpu.CompilerParams(dimension_semantics=("parallel", "arbitrary"),
                                             vmem_limit_bytes=VMEM_LIMIT),
        name="moba",
    )(aq, ak, av, tiles, far)


def _merge_kernel(x_ref, ys_ref, ym_ref, ya_ref, gate_ref, wps_ref, wpm_ref, wpa_ref, wo_ref, g_ref, o_ref):
    def gate(i):
        return gate_ref[:, i * D_MODEL:(i + 1) * D_MODEL].astype(F32)

    merged = (gate(0) * _dot(ys_ref[...].astype(BF16), wps_ref[...])
              + gate(1) * _dot(ym_ref[...].astype(BF16), wpm_ref[...])
              + gate(2) * _dot(ya_ref[...].astype(BF16), wpa_ref[...]))
    z = _dot(merged.astype(BF16), wo_ref[...])
    o_ref[...] = x_ref[...] + _rms(z, g_ref[...])


def _merge(x2d, ys, ym, ya, gate, wps, wpm, wpa, wo, g, ts):
    t = x2d.shape[0]
    row = lambda i: (i, 0)
    full = lambda i: (0, 0)
    return pl.pallas_call(
        _merge_kernel,
        grid=(t // ts,),
        in_specs=[pl.BlockSpec((ts, D_MODEL), row),
                  pl.BlockSpec((ts, SSM_WIDTH), row),
                  pl.BlockSpec((ts, MLSTM_WIDTH), row),
                  pl.BlockSpec((ts, MOBA_WIDTH), row),
                  pl.BlockSpec((ts, 3 * D_MODEL), row),
                  pl.BlockSpec(wps.shape, full),
                  pl.BlockSpec(wpm.shape, full),
                  pl.BlockSpec(wpa.shape, full),
                  pl.BlockSpec(wo.shape, full),
                  pl.BlockSpec((1, D_MODEL), full)],
        out_specs=pl.BlockSpec((ts, D_MODEL), row),
        out_shape=jax.ShapeDtypeStruct((t, D_MODEL), F32),
        compiler_params=pltpu.CompilerParams(dimension_semantics=("parallel",),
                                             vmem_limit_bytes=VMEM_LIMIT),
        name="merge",
    )(x2d, ys, ym, ya, gate, wps, wpm, wpa, wo, g)


def _ffn_kernel(x_ref, g1_ref, w1_ref, w2_ref, g2_ref, o_ref):
    x = x_ref[...]
    hb = _rms(x, g1_ref[...]).astype(BF16)
    a = jnp.maximum(_dot(hb, w1_ref[...]), 0.0)
    f = _dot((a * a).astype(BF16), w2_ref[...])
    o_ref[...] = x + _rms(f, g2_ref[...])


def _ffn(x2d, g1, w1, w2, g2, ts):
    t = x2d.shape[0]
    full = lambda i: (0, 0)
    return pl.pallas_call(
        _ffn_kernel,
        grid=(t // ts,),
        in_specs=[pl.BlockSpec((ts, D_MODEL), lambda i: (i, 0)),
                  pl.BlockSpec((1, D_MODEL), full),
                  pl.BlockSpec((D_MODEL, D_FF), full, pipeline_mode=pl.Buffered(1)),
                  pl.BlockSpec((D_FF, D_MODEL), full, pipeline_mode=pl.Buffered(1)),
                  pl.BlockSpec((1, D_MODEL), full)],
        out_specs=pl.BlockSpec((ts, D_MODEL), lambda i: (i, 0)),
        out_shape=jax.ShapeDtypeStruct((t, D_MODEL), F32),
        compiler_params=pltpu.CompilerParams(dimension_semantics=("parallel",),
                                             vmem_limit_bytes=VMEM_LIMIT),
        name="ffn",
    )(x2d, g1, w1, w2, g2)


def _split_w_in(w_in):
    n_if = 2 * MLSTM_HEADS
    c_if = SSM_WIDTH + 4 * MLSTM_WIDTH
    assert c_if == C_AQ and w_in.shape[-1] - n_if == C_GIF
    wa = w_in[..., :c_if].astype(BF16)
    wb = w_in[..., c_if + n_if:].astype(BF16)
    wg = jnp.pad(w_in[..., c_if:c_if + n_if], ((0, 0), (0, 0), (0, LANES - n_if))).astype(BF16)
    return wa, wb, wg


def kernel(x, w_in, conv_w, ssm_a_re, ssm_a_im, ssm_log_dt, ssm_b_re, ssm_b_im, ssm_c_re, ssm_c_im, ssm_d, ssm_w_glu, mlstm_i_bias, mlstm_f_bias, mlstm_head_gain, rel_bias, w_ssm_proj, w_mlstm_proj, w_moba_proj, w_out, w_ff1, w_ff2, norm_mix_pre, norm_mix_post, norm_ffn_pre, norm_ffn_post):
    bsz, seq, _ = x.shape
    depth = w_in.shape[0]
    assert bsz == SUBLANES and seq % MOBA_BLOCK == 0 and seq // MOBA_BLOCK <= SUBLANES
    ts_in = 256
    ts = 512
    tc = 64
    x2d = x.reshape(bsz * seq, D_MODEL)
    wa, wb, wg = _split_w_in(w_in)
    wps, wpm, wpa = w_ssm_proj.astype(BF16), w_mlstm_proj.astype(BF16), w_moba_proj.astype(BF16)
    wo = w_out.astype(BF16)
    w1 = w_ff1.astype(BF16)
    w2 = w_ff2.astype(BF16)
    wglu = ssm_w_glu.astype(BF16)
    gate_bias = jnp.pad(jnp.concatenate([mlstm_i_bias, mlstm_f_bias], axis=-1).astype(F32),
                        ((0, 0), (0, LANES - 2 * MLSTM_HEADS)))
    tiles, far = _moba_bias_tiles(rel_bias)
    for l in range(depth):
        u, qk, mv, mo, aq, ak, av, gate, gif = _inproj(x2d, norm_mix_pre[l][None, :], wa[l], wb[l], wg[l],
                                                       ts_in)
        bblk, cblk, ar_b, ai_b = _ssm_params(ssm_a_re[l], ssm_a_im[l], ssm_log_dt[l], ssm_b_re[l], ssm_b_im[l],
                                             ssm_c_re[l], ssm_c_im[l], bsz)
        y_ssm = _ssm(u.reshape(bsz, seq, SSM_WIDTH), bblk, cblk, ar_b, ai_b, ssm_d[l][None, :], wglu[l],
                     bsz, seq, tc)
        y_mlstm = _mlstm(qk, mv, mo, gif, conv_w[l].astype(F32), gate_bias[l][None, :],
                         mlstm_head_gain[l][None, :].astype(F32), bsz, seq)
        y_moba = _moba(aq, ak, av, tiles, far, bsz, seq)
        x2d = _merge(x2d, y_ssm.reshape(bsz * seq, SSM_WIDTH), y_mlstm, y_moba, gate, wps[l], wpm[l], wpa[l],
                     wo[l], norm_mix_post[l][None, :], ts)
        x2d = _ffn(x2d, norm_ffn_pre[l][None, :], w1[l], w2[l], norm_ffn_post[l][None, :], ts)
    return x2d.reshape(bsz, seq, D_MODEL)
```

```python
import functools
import math

import jax
import jax.numpy as jnp
from jax import lax
from jax.experimental import pallas as pl
from jax.experimental.pallas import tpu as pltpu

F32 = jnp.float32
BF16 = jnp.bfloat16

D_MODEL = 1024
SSM_WIDTH = 512
SSM_GROUP = 16
SSM_GROUPS = 32
SSM_STATE = 64
MLSTM_WIDTH = 512
MLSTM_HEADS = 4
MLSTM_HEAD_DIM = 128
MLSTM_CHUNK = 128
MLSTM_CHUNKS_PER_STEP = 4
CONV_WIDTH = 4
MOBA_WIDTH = 512
MOBA_HEADS = 8
MOBA_HEAD_DIM = 64
MOBA_BLOCK = 256
MOBA_TOPK = 3
REL_BUCKETS = 32
REL_MAX_DIST = 128
D_FF = 4 * D_MODEL
RMS_EPS = 1e-6
NEG_INF = -1e30

LANES = 128
SUBLANES = 8
VMEM_LIMIT = 56 * 1024 * 1024

C_U = 0
C_QK = 512
C_MV = 1536
C_MO = 2048
C_AQ = 2560
C_AK = 3072
C_AV = 3584
C_GATE = 4096
C_GIF = 7168
C_END = 7296


def _rms(x, g):
    return x * lax.rsqrt(jnp.mean(x * x, axis=-1, keepdims=True) + RMS_EPS) * g


def _sigmoid(x):
    return 1.0 / (1.0 + jnp.exp(-x))


def _dot(a, b):
    return jnp.dot(a, b, preferred_element_type=F32)


def _dot_nt(a, b):
    return lax.dot_general(a, b, (((1,), (1,)), ((), ())), preferred_element_type=F32)


def _split3(a):
    hi = a.astype(BF16)
    r1 = a - hi.astype(F32)
    mid = r1.astype(BF16)
    lo = (r1 - mid.astype(F32)).astype(BF16)
    return hi, mid, lo


def _inproj_kernel(x_ref, g_ref, wa_ref, wb_ref, wg_ref, u_ref, qk_ref, mv_ref, mo_ref, aq_ref, ak_ref, av_ref,
                   gate_ref, gif_ref):
    hb = _rms(x_ref[...], g_ref[...]).astype(BF16)

    def proj(w_ref, a, b):
        return _dot(hb, w_ref[:, a:b])

    u_ref[...] = proj(wa_ref, C_U, C_QK)
    qk_ref[...] = proj(wa_ref, C_QK, C_MV)
    mv_ref[...] = proj(wa_ref, C_MV, C_MO)
    mo_ref[...] = proj(wa_ref, C_MO, C_AQ)
    aq_ref[...] = proj(wb_ref, C_AQ - C_AQ, C_AK - C_AQ)
    ak_ref[...] = proj(wb_ref, C_AK - C_AQ, C_AV - C_AQ)
    av_ref[...] = proj(wb_ref, C_AV - C_AQ, C_GATE - C_AQ)
    gate_ref[...] = _sigmoid(proj(wb_ref, C_GATE - C_AQ, C_GIF - C_AQ)).astype(BF16)
    gif_ref[...] = _dot(hb, wg_ref[...])


def _inproj(x2d, g, wa, wb, wg, ts):
    t = x2d.shape[0]
    row = lambda i: (i, 0)
    full = lambda i: (0, 0)

    def out(n, dtype=F32):
        return jax.ShapeDtypeStruct((t, n), dtype), pl.BlockSpec((ts, n), row)

    outs = [out(512), out(1024), out(512), out(512), out(512), out(512), out(512), out(3 * D_MODEL, BF16),
            out(LANES)]
    return pl.pallas_call(
        _inproj_kernel,
        grid=(t // ts,),
        in_specs=[pl.BlockSpec((ts, D_MODEL), row),
                  pl.BlockSpec((1, D_MODEL), full),
                  pl.BlockSpec(wa.shape, full, pipeline_mode=pl.Buffered(1)),
                  pl.BlockSpec(wb.shape, full, pipeline_mode=pl.Buffered(1)),
                  pl.BlockSpec(wg.shape, full, pipeline_mode=pl.Buffered(1))],
        out_specs=[o[1] for o in outs],
        out_shape=[o[0] for o in outs],
        compiler_params=pltpu.CompilerParams(dimension_semantics=("parallel",),
                                             vmem_limit_bytes=VMEM_LIMIT),
        name="inproj",
    )(x2d, g, wa, wb, wg)


SSM_KB = 2
SSM_KB_CH = SSM_WIDTH // SSM_KB
SSM_KB_ST = SSM_KB_CH // SSM_GROUP * SSM_STATE
SSM_SLAB = 1024


def _gelu(x):
    return 0.5 * x * (1.0 + jnp.tanh(math.sqrt(2.0 / math.pi) * (x + 0.044715 * (x * x * x))))


def _ssm_kernel(u_ref, perm_ref, perm_t_ref, bblk_ref, cblk_ref, ar_ref, ai_ref, d_ref, wglu_ref, o_ref,
                xs_ref, st_ref, *, bsz, tc):
    @pl.when(pl.program_id(0) == 0)
    def _():
        st_ref[...] = jnp.zeros_like(st_ref)

    u_bm = u_ref[...].reshape(bsz * tc, SSM_WIDTH)
    u_hi = u_bm.astype(BF16)
    u_lo = (u_bm - u_hi.astype(F32)).astype(BF16)
    u_tm_hi = _dot(perm_ref[...], u_hi)
    u = u_tm_hi + _dot(perm_ref[...], u_lo)
    ub = u_tm_hi.astype(BF16)
    for kb in range(SSM_KB):
        xs_ref[:, kb * 2 * SSM_KB_ST:(kb + 1) * 2 * SSM_KB_ST] = _dot(
            ub[:, kb * SSM_KB_CH:(kb + 1) * SSM_KB_CH], bblk_ref[kb])

    for kb in range(SSM_KB):
        for j in range(SSM_KB_ST // SSM_SLAB):
            re0 = kb * 2 * SSM_KB_ST + j * SSM_SLAB
            im0 = re0 + SSM_KB_ST
            co = kb * SSM_KB_ST + j * SSM_SLAB

            def body(t, carry, re0=re0, im0=im0, co=co):
                xr, xi = carry
                ar = ar_ref[:, co:co + SSM_SLAB]
                ai = ai_ref[:, co:co + SSM_SLAB]
                r0 = pl.multiple_of(t * bsz, bsz)
                br = xs_ref[pl.ds(r0, bsz), re0:re0 + SSM_SLAB]
                bi = xs_ref[pl.ds(r0, bsz), im0:im0 + SSM_SLAB]
                nr = ar * xr - ai * xi + br
                ni = ar * xi + ai * xr + bi
                xs_ref[pl.ds(r0, bsz), re0:re0 + SSM_SLAB] = nr
                xs_ref[pl.ds(r0, bsz), im0:im0 + SSM_SLAB] = ni
                return nr, ni

            xr, xi = lax.fori_loop(0, tc, body,
                                   (st_ref[:, re0:re0 + SSM_SLAB], st_ref[:, im0:im0 + SSM_SLAB]),
                                   unroll=8)
            st_ref[:, re0:re0 + SSM_SLAB] = xr
            st_ref[:, im0:im0 + SSM_SLAB] = xi

    ys = []
    for kb in range(SSM_KB):
        xb = xs_ref[:, kb * 2 * SSM_KB_ST:(kb + 1) * 2 * SSM_KB_ST].astype(BF16)
        ys.append(_dot(xb, cblk_ref[kb]))
    y = jnp.concatenate(ys, axis=-1) + d_ref[...] * u
    yg = _gelu(y)
    out_tm = (yg * _sigmoid(_dot(yg.astype(BF16), wglu_ref[...]))).astype(BF16)
    o_ref[...] = _dot(perm_t_ref[...], out_tm).astype(BF16).reshape(bsz, tc, SSM_WIDTH)


def _ssm(u3, bblk, cblk, ar, ai, d, wglu, bsz, seq, tc):
    rows = tc * bsz
    full2 = lambda c: (0, 0)
    full3 = lambda c: (0, 0, 0)
    r = jnp.arange(rows)
    perm = (r[None, :] == (r[:, None] % bsz) * tc + r[:, None] // bsz).astype(BF16)
    return pl.pallas_call(
        functools.partial(_ssm_kernel, bsz=bsz, tc=tc),
        grid=(seq // tc,),
        in_specs=[pl.BlockSpec((bsz, tc, SSM_WIDTH), lambda c: (0, c, 0)),
                  pl.BlockSpec((rows, rows), full2),
                  pl.BlockSpec((rows, rows), full2),
                  pl.BlockSpec(bblk.shape, full3),
                  pl.BlockSpec(cblk.shape, full3),
                  pl.BlockSpec(ar.shape, full2),
                  pl.BlockSpec(ai.shape, full2),
                  pl.BlockSpec((1, SSM_WIDTH), full2),
                  pl.BlockSpec((SSM_WIDTH, SSM_WIDTH), full2)],
        out_specs=pl.BlockSpec((bsz, tc, SSM_WIDTH), lambda c: (0, c, 0)),
        out_shape=jax.ShapeDtypeStruct((bsz, seq, SSM_WIDTH), BF16),
        scratch_shapes=[pltpu.VMEM((rows, SSM_KB * 2 * SSM_KB_ST), F32),
                        pltpu.VMEM((bsz, SSM_KB * 2 * SSM_KB_ST), F32)],
        compiler_params=pltpu.CompilerParams(dimension_semantics=("arbitrary",),
                                             vmem_limit_bytes=VMEM_LIMIT),
        name="ssm",
    )(u3, perm, perm.T, bblk, cblk, ar, ai, d, wglu)


def _ssm_params(a_re, a_im, log_dt, b_re, b_im, c_re, c_im, bsz):
    ar, ai = a_re.astype(F32), a_im.astype(F32)
    dt = jnp.exp(log_dt.astype(F32))[:, None]
    decay = jnp.exp(dt * ar)
    abar_r, abar_i = decay * jnp.cos(dt * ai), decay * jnp.sin(dt * ai)
    den = ar * ar + ai * ai
    nr, ni = abar_r - 1.0, abar_i
    fr, fi = (nr * ar + ni * ai) / den, (ni * ar - nr * ai) / den
    br, bi = b_re.astype(F32), b_im.astype(F32)
    bbar_r = fr[..., None] * br - fi[..., None] * bi
    bbar_i = fr[..., None] * bi + fi[..., None] * br
    gl = SSM_GROUPS // SSM_KB
    eye = jnp.eye(gl, dtype=F32)

    def pack_b(bb):
        bb = bb.reshape(SSM_KB, gl, SSM_STATE, SSM_GROUP)
        dense = jnp.einsum('kgph,gq->kghqp', bb, eye)
        return dense.reshape(SSM_KB, gl * SSM_GROUP, gl * SSM_STATE)

    def pack_c(cc):
        cc = cc.reshape(SSM_KB, gl, SSM_GROUP, SSM_STATE)
        dense = jnp.einsum('kghp,gq->kgpqh', cc, eye)
        return dense.reshape(SSM_KB, gl * SSM_STATE, gl * SSM_GROUP)

    bblk = jnp.concatenate([pack_b(bbar_r), pack_b(bbar_i)], axis=-1).astype(BF16)
    cblk = jnp.concatenate([pack_c(c_re.astype(F32)), pack_c(-c_im.astype(F32))], axis=1).astype(BF16)
    ar_b = jnp.broadcast_to(abar_r.reshape(1, -1), (bsz, SSM_GROUPS * SSM_STATE))
    ai_b = jnp.broadcast_to(abar_i.reshape(1, -1), (bsz, SSM_GROUPS * SSM_STATE))
    return bblk, cblk, ar_b, ai_b


def _mlstm_kernel(qk_ref, v_ref, o_ref, gif_ref, convw_ref, bias_ref, gain_ref, tri_ref, spread_ref, out_ref,
                  xp_ref, ct_ref, n_ref, m_ref):
    lc, dh, nh = MLSTM_CHUNK, MLSTM_HEAD_DIM, MLSTM_HEADS
    halo = SUBLANES

    @pl.when(pl.program_id(1) == 0)
    def _():
        xp_ref[0:halo, :] = jnp.zeros((halo, 2 * MLSTM_WIDTH), F32)
        ct_ref[...] = jnp.zeros_like(ct_ref)
        n_ref[...] = jnp.zeros_like(n_ref)
        m_ref[...] = jnp.zeros_like(m_ref)

    rows = qk_ref.shape[0]
    xp_ref[halo:halo + rows, :] = qk_ref[...]
    conv = convw_ref[CONV_WIDTH - 1:CONV_WIDTH, :] * xp_ref[halo:halo + rows, :]
    for j in range(CONV_WIDTH - 1):
        off = halo - (CONV_WIDTH - 1) + j
        conv = conv + convw_ref[j:j + 1, :] * xp_ref[off:off + rows, :]
    xp_ref[0:halo, :] = xp_ref[rows:rows + halo, :]
    qk_all = conv * _sigmoid(conv)

    hs = range(nh)
    state = ([m_ref[h:h + 1, :] for h in hs],
             [n_ref[h:h + 1, :] for h in hs],
             [ct_ref[h] for h in hs])
    for c in range(rows // lc):
        r0 = c * lc
        state = _mlstm_chunk(qk_all[r0:r0 + lc], v_ref[r0:r0 + lc, :], o_ref[r0:r0 + lc, :],
                             gif_ref[r0:r0 + lc, :] + bias_ref[...], gain_ref[...], tri_ref, spread_ref[...],
                             state, out_ref.at[r0:r0 + lc, :])
    for h in hs:
        m_ref[h:h + 1, :] = state[0][h]
        n_ref[h:h + 1, :] = state[1][h]
        ct_ref[h] = state[2][h]


def _mlstm_chunk(qk, v_all, o_all, gpre, gain, tri_ref, spread, state, out_ref):
    lc, dh, nh = MLSTM_CHUNK, MLSTM_HEAD_DIM, MLSTM_HEADS
    m_prev, n_prev, ct_prev = state
    lsig = jnp.minimum(gpre, 0.0) - jnp.log(1.0 + jnp.exp(-jnp.abs(gpre)))
    row_i = lax.broadcasted_iota(jnp.int32, (lc, lc), 0)
    col_i = lax.broadcasted_iota(jnp.int32, (lc, lc), 1)
    causal = col_i <= row_i
    tri = tri_ref[0]
    tri_t = tri_ref[1]
    l_hi, l_mid, l_lo = _split3(lsig)
    bcum_c = _dot(tri, l_hi) + _dot(tri, l_mid) + _dot(tri, l_lo)
    lsig_t = lsig.T
    t_hi, t_mid, t_lo = _split3(lsig_t)
    bcum_r = _dot(t_hi, tri_t) + _dot(t_mid, tri_t) + _dot(t_lo, tri_t)
    gpre_t = gpre.T
    per_time = jnp.where(col_i < nh, gpre, bcum_c)
    p_hi, p_mid, p_lo = _split3(per_time)
    rep = _dot(p_hi, spread) + _dot(p_mid, spread) + _dot(p_lo, spread)

    hs = range(nh)
    qf = [qk[:, h * dh:(h + 1) * dh] for h in hs]
    q = [x.astype(BF16) for x in qf]
    k = [qk[:, MLSTM_WIDTH + h * dh:MLSTM_WIDTH + (h + 1) * dh] * (dh ** -0.5) for h in hs]
    kb = [x.astype(BF16) for x in k]
    v = [v_all[:, h * dh:(h + 1) * dh] for h in hs]
    bc = [rep[:, (nh + h) * LANES:(nh + h + 1) * LANES] for h in hs]
    brow = [bcum_r[nh + h:nh + h + 1, :] for h in hs]
    ic = [rep[:, h * LANES:(h + 1) * LANES] for h in hs]
    irow = [gpre_t[h:h + 1, :] for h in hs]
    gtot = [x[lc - 1:lc, :] for x in bc]

    qk_t = [_dot_nt(q[h], kb[h]) for h in hs]
    q_ct = [_dot(q[h], ct_prev[h].astype(BF16)) for h in hs]
    m_loc = [jnp.max(gtot[h] - brow[h] + irow[h], axis=-1, keepdims=True) for h in hs]
    wgt_c = [jnp.exp(gtot[h] - bc[h] + ic[h] - m_loc[h]) for h in hs]
    d_ct = [_dot(k[h].T.astype(BF16), (wgt_c[h] * v[h]).astype(BF16)) for h in hs]

    log_d = [jnp.where(causal, bc[h] - brow[h] + irow[h], NEG_INF) for h in hs]
    m_intra = [jnp.max(log_d[h], axis=-1, keepdims=True) for h in hs]
    qn = [jnp.sum(qf[h] * n_prev[h], axis=-1, keepdims=True) for h in hs]
    log_inter = [bc[h] + m_prev[h] for h in hs]
    m_q = [jnp.maximum(log_inter[h], m_intra[h]) for h in hs]
    s_mat = [qk_t[h] * jnp.exp(log_d[h] - m_q[h]) for h in hs]
    s_sum = [jnp.sum(s_mat[h], axis=-1, keepdims=True) for h in hs]
    s_v = [_dot(s_mat[h].astype(BF16), v[h].astype(BF16)) for h in hs]
    inter = [jnp.exp(log_inter[h] - m_q[h]) for h in hs]
    hh = [(inter[h] * q_ct[h] + s_v[h])
          / jnp.maximum(jnp.abs(s_sum[h] + inter[h] * qn[h]), jnp.exp(-m_q[h])) for h in hs]
    ms = [jnp.mean(hh[h] * hh[h], axis=-1, keepdims=True) for h in hs]
    for h in hs:
        og = _sigmoid(o_all[:, h * dh:(h + 1) * dh])
        out_ref[:, h * dh:(h + 1) * dh] = og * (hh[h] * lax.rsqrt(ms[h] + RMS_EPS)
                                                * gain[:, h * dh:(h + 1) * dh])

    m_next, n_next, ct_next = [], [], []
    for h in hs:
        d_n = jnp.sum(wgt_c[h] * k[h], axis=0, keepdims=True)
        m_new = jnp.maximum(gtot[h] + m_prev[h], m_loc[h])
        a = jnp.exp(gtot[h] + m_prev[h] - m_new)
        bb = jnp.exp(m_loc[h] - m_new)
        ct_next.append(a * ct_prev[h] + bb * d_ct[h])
        n_next.append(a * n_prev[h] + bb * d_n)
        m_next.append(m_new)
    return m_next, n_next, ct_next


def _mlstm(qk, mv, mo, gif, convw, bias, gain, bsz, seq):
    t = bsz * seq
    rows = MLSTM_CHUNKS_PER_STEP * MLSTM_CHUNK
    nc = seq // rows
    row = lambda b, c: (b * nc + c, 0)
    full = lambda b, c: (0, 0)
    r = jnp.arange(MLSTM_CHUNK)
    lower = r[None, :] <= r[:, None]
    tri = jnp.stack([lower, lower.T]).astype(BF16)
    spread = (r[:, None] == jnp.arange(2 * MLSTM_HEADS * LANES)[None, :] // LANES).astype(BF16)
    lc = rows
    return pl.pallas_call(
        _mlstm_kernel,
        grid=(bsz, nc),
        in_specs=[pl.BlockSpec((lc, 2 * MLSTM_WIDTH), row),
                  pl.BlockSpec((lc, MLSTM_WIDTH), row),
                  pl.BlockSpec((lc, MLSTM_WIDTH), row),
                  pl.BlockSpec((lc, LANES), row),
                  pl.BlockSpec((CONV_WIDTH, 2 * MLSTM_WIDTH), full),
                  pl.BlockSpec((1, LANES), full),
                  pl.BlockSpec((1, MLSTM_WIDTH), full),
                  pl.BlockSpec(tri.shape, lambda b, c: (0, 0, 0)),
                  pl.BlockSpec(spread.shape, full)],
        out_specs=pl.BlockSpec((lc, MLSTM_WIDTH), row),
        out_shape=jax.ShapeDtypeStruct((t, MLSTM_WIDTH), F32),
        scratch_shapes=[pltpu.VMEM((lc + SUBLANES, 2 * MLSTM_WIDTH), F32),
                        pltpu.VMEM((MLSTM_HEADS, MLSTM_HEAD_DIM, MLSTM_HEAD_DIM), F32),
                        pltpu.VMEM((SUBLANES, MLSTM_HEAD_DIM), F32),
                        pltpu.VMEM((SUBLANES, LANES), F32)],
        compiler_params=pltpu.CompilerParams(dimension_semantics=("parallel", "arbitrary"),
                                             vmem_limit_bytes=VMEM_LIMIT),
        name="mlstm",
    )(qk, mv, mo, gif, convw, bias, gain, tri, spread)


def _rel_bucket(dist):
    max_exact = REL_BUCKETS // 2
    is_small = dist < max_exact
    large = max_exact + (jnp.log(jnp.maximum(dist, 1).astype(F32) / max_exact)
                         / math.log(REL_MAX_DIST / max_exact) * (REL_BUCKETS - max_exact)).astype(jnp.int32)
    large = jnp.minimum(large, REL_BUCKETS - 1)
    return jnp.where(is_small, dist, large)


LOG2E = math.log2(math.e)
MOBA_LOOKAHEAD = 8


def _moba_bias_tiles(rel_bias):
    bs = MOBA_BLOCK
    d_own = jnp.arange(bs)[None, :] - jnp.arange(bs)[:, None]
    tab = rel_bias.astype(F32) * LOG2E
    buckets = jnp.stack([_rel_bucket(jnp.maximum(d_own, 0)), _rel_bucket(bs + d_own)])
    onehot = (buckets[..., None] == jnp.arange(REL_BUCKETS)).astype(F32)
    tiles = jnp.einsum('tjib,hb->htji', onehot, tab, precision=lax.Precision.HIGHEST)
    causal = jnp.stack([d_own >= 0, jnp.ones_like(d_own, dtype=bool)])
    tiles = jnp.where(causal[None], tiles, NEG_INF)
    far = jnp.take(tab, _rel_bucket(jnp.full((1,), 2 * bs, jnp.int32)), axis=1)
    far_rows = jnp.broadcast_to(far[:, None, :], (MOBA_HEADS, SUBLANES, LANES)).reshape(-1, LANES)
    return tiles, jnp.pad(far_rows, ((0, LANES - MOBA_HEADS * SUBLANES), (0, 0)))


def _moba_kernel(q_ref, k_ref, v_ref, tiles_ref, far_ref, out_ref,
                 kaug_ref, vt_ref, km_ref, qaug_ref, *, nb):
    bs, nh, dh = MOBA_BLOCK, MOBA_HEADS, MOBA_HEAD_DIM
    qi = pl.program_id(1)

    @pl.when(qi == 0)
    def _():
        lane = lax.broadcasted_iota(jnp.int32, (bs, LANES), 1)
        means = []
        for n in range(nb):
            kblk = k_ref[n * bs:(n + 1) * bs, :]
            means.append(jnp.mean(kblk, axis=0, keepdims=True))
            for h in range(nh):
                p, e = divmod(h, 2)
                o = (1 - e) * dh
                hot = jnp.where(lane == o + n, 1.0, jnp.where(lane == o + SUBLANES + n, 1.0, 0.0))
                kaug_ref[h, n] = jnp.where(lane // dh == e, kblk[:, p * LANES:(p + 1) * LANES], hot).astype(BF16)
            vt_ref[n] = v_ref[n * bs:(n + 1) * bs, :].T.astype(BF16)
        means += [jnp.zeros_like(means[0])] * (SUBLANES - nb)
        kmean = jnp.concatenate(means, axis=0)
        rows = lax.broadcasted_iota(jnp.int32, (LANES, MOBA_WIDTH), 0)
        cols = lax.broadcasted_iota(jnp.int32, (LANES, MOBA_WIDTH), 1)
        tiled = jnp.concatenate([kmean] * (LANES // SUBLANES), axis=0)
        km_ref[...] = jnp.where(rows // SUBLANES == cols // dh, tiled, 0.0)

    q = q_ref[...]
    q_hi, q_mid, _ = _split3(q)
    m_hi, m_mid, _ = _split3(km_ref[...])
    gate = _dot_nt(m_hi, q_hi) + _dot_nt(m_mid, q_hi) + _dot_nt(m_hi, q_mid)
    nrows = LANES
    blk = lax.broadcasted_iota(jnp.int32, (nrows, bs), 0) % SUBLANES
    gate = jnp.where(blk < qi, gate, NEG_INF)
    rank = jnp.zeros((nrows, bs), F32)
    for s in range(1, SUBLANES):
        fwd = pltpu.roll(gate, nrows - s, 0)
        back = pltpu.roll(gate, SUBLANES - s, 0)
        wrapped = blk + s >= SUBLANES
        other = jnp.where(wrapped, back, fwd)
        rank = rank + jnp.where(wrapped, jnp.where(other >= gate, 1.0, 0.0), jnp.where(other > gate, 1.0, 0.0))
    chosen = rank < MOBA_TOPK
    far = jnp.concatenate([far_ref[...]] * (bs // LANES), axis=1)
    far_hi = far.astype(BF16).astype(F32)
    far_lo = far - far_hi
    sel_hi = jnp.where(blk < qi - 1, jnp.where(chosen, far_hi, NEG_INF),
                       jnp.where(blk == qi - 1, jnp.where(chosen, 0.0, NEG_INF),
                                 jnp.where(blk == qi, 0.0, NEG_INF)))
    sel_lo = jnp.where(blk < qi - 1, jnp.where(chosen, far_lo, 0.0), 0.0)

    q_t = (q * (dh ** -0.5 * LOG2E)).T
    pad = jnp.zeros((dh - 2 * SUBLANES, bs), F32)
    for h in range(nh):
        e = h % 2
        extra = [sel_hi[h * SUBLANES:(h + 1) * SUBLANES], sel_lo[h * SUBLANES:(h + 1) * SUBLANES], pad]
        q_h = [q_t[h * dh:(h + 1) * dh]]
        qaug_ref[h] = jnp.concatenate(q_h + extra if e == 0 else extra + q_h, axis=0).astype(BF16)

    def scores(h, n):
        return _dot(kaug_ref[h, n], qaug_ref[h])

    def absorb(h, n, s, state):
        mx = jnp.max(s, axis=0, keepdims=True)
        v_t = vt_ref[n, h * dh:(h + 1) * dh, :]
        m_old, l_old, acc = state
        m_new = jnp.maximum(m_old, mx)
        alpha = jnp.exp2(m_old - m_new)
        pr = jnp.exp2(s - m_new)
        return (m_new, alpha * l_old + jnp.sum(pr, axis=0, keepdims=True),
                alpha * acc + _dot(v_t, pr.astype(BF16)))

    def run_tiles(jobs, states):
        states = list(states)
        pending = {}
        for i in range(min(MOBA_LOOKAHEAD, len(jobs))):
            pending[i] = scores(*jobs[i][:2])
        for i, (h, n, tile) in enumerate(jobs):
            s = pending.pop(i)
            if tile is not None:
                s = tile + s
            states[h] = absorb(h, n, s, states[h])
            if i + MOBA_LOOKAHEAD < len(jobs):
                pending[i + MOBA_LOOKAHEAD] = scores(*jobs[i + MOBA_LOOKAHEAD][:2])
        return tuple(states)

    states = tuple((jnp.full((1, bs), NEG_INF, F32), jnp.zeros((1, bs), F32), jnp.zeros((dh, bs), F32))
                   for _ in range(nh))

    def near_body(k, carry):
        return run_tiles([(h, qi - k, tiles_ref[h, k]) for h in range(nh)], carry)

    def far_body(n, carry):
        return run_tiles([(h, n, None) for h in range(nh)], carry)

    states = lax.fori_loop(0, jnp.minimum(qi + 1, 2), near_body, states)
    states = lax.fori_loop(0, jnp.maximum(qi - 1, 0), far_body, states)
    out_t = jnp.concatenate([acc * (1.0 / l) for (_, l, acc) in states], axis=0)
    out_ref[...] = out_t.T


def _moba(aq, ak, av, tiles, far, bsz, seq):
    t = bsz * seq
    bs = MOBA_BLOCK
    nb = seq // bs
    return pl.pallas_call(
        functools.partial(_moba_kernel, nb=nb),
        grid=(bsz, nb),
        in_specs=[pl.BlockSpec((bs, MOBA_WIDTH), lambda b, i: (b * nb + i, 0)),
                  pl.BlockSpec((seq, MOBA_WIDTH), lambda b, i: (b, 0)),
                  pl.BlockSpec((seq, MOBA_WIDTH), lambda b, i: (b, 0)),
                  pl.BlockSpec(tiles.shape, lambda b, i: (0, 0, 0, 0)),
                  pl.BlockSpec(far.shape, lambda b, i: (0, 0))],
        out_specs=pl.BlockSpec((bs, MOBA_WIDTH), lambda b, i: (b * nb + i, 0)),
        out_shape=jax.ShapeDtypeStruct((t, MOBA_WIDTH), F32),
        scratch_shapes=[pltpu.VMEM((MOBA_HEADS, nb, bs, LANES), BF16),
                        pltpu.VMEM((nb, MOBA_WIDTH, bs), BF16),
                        pltpu.VMEM((LANES, MOBA_WIDTH), F32),
                        pltpu.VMEM((MOBA_HEADS, LANES, bs), BF16)],
        compiler_params=pltpu.CompilerParams(dimension_semantics=("parallel", "arbitrary"),
                                             vmem_limit_bytes=VMEM_LIMIT),
        name="moba",
    )(aq, ak, av, tiles, far)


def _merge_kernel(x_ref, ys_ref, ym_ref, ya_ref, gate_ref, wps_ref, wpm_ref, wpa_ref, wo_ref, g_ref, o_ref):
    def gate(i):
        return gate_ref[:, i * D_MODEL:(i + 1) * D_MODEL].astype(F32)

    merged = (gate(0) * _dot(ys_ref[...].astype(BF16), wps_ref[...])
              + gate(1) * _dot(ym_ref[...].astype(BF16), wpm_ref[...])
              + gate(2) * _dot(ya_ref[...].astype(BF16), wpa_ref[...]))
    z = _dot(merged.astype(BF16), wo_ref[...])
    o_ref[...] = x_ref[...] + _rms(z, g_ref[...])


def _merge(x2d, ys, ym, ya, gate, wps, wpm, wpa, wo, g, ts):
    t = x2d.shape[0]
    row = lambda i: (i, 0)
    full = lambda i: (0, 0)
    return pl.pallas_call(
        _merge_kernel,
        grid=(t // ts,),
        in_specs=[pl.BlockSpec((ts, D_MODEL), row),
                  pl.BlockSpec((ts, SSM_WIDTH), row),
                  pl.BlockSpec((ts, MLSTM_WIDTH), row),
                  pl.BlockSpec((ts, MOBA_WIDTH), row),
                  pl.BlockSpec((ts, 3 * D_MODEL), row),
                  pl.BlockSpec(wps.shape, full),
                  pl.BlockSpec(wpm.shape, full),
                  pl.BlockSpec(wpa.shape, full),
                  pl.BlockSpec(wo.shape, full),
                  pl.BlockSpec((1, D_MODEL), full)],
        out_specs=pl.BlockSpec((ts, D_MODEL), row),
        out_shape=jax.ShapeDtypeStruct((t, D_MODEL), F32),
        compiler_params=pltpu.CompilerParams(dimension_semantics=("parallel",),
                                             vmem_limit_bytes=VMEM_LIMIT),
        name="merge",
    )(x2d, ys, ym, ya, gate, wps, wpm, wpa, wo, g)


def _ffn_kernel(x_ref, g1_ref, w1_ref, w2_ref, g2_ref, o_ref):
    x = x_ref[...]
    hb = _rms(x, g1_ref[...]).astype(BF16)
    a = jnp.maximum(_dot(hb, w1_ref[...]), 0.0)
    f = _dot((a * a).astype(BF16), w2_ref[...])
    o_ref[...] = x + _rms(f, g2_ref[...])


def _ffn(x2d, g1, w1, w2, g2, ts):
    t = x2d.shape[0]
    full = lambda i: (0, 0)
    return pl.pallas_call(
        _ffn_kernel,
        grid=(t // ts,),
        in_specs=[pl.BlockSpec((ts, D_MODEL), lambda i: (i, 0)),
                  pl.BlockSpec((1, D_MODEL), full),
                  pl.BlockSpec((D_MODEL, D_FF), full, pipeline_mode=pl.Buffered(1)),
                  pl.BlockSpec((D_FF, D_MODEL), full, pipeline_mode=pl.Buffered(1)),
                  pl.BlockSpec((1, D_MODEL), full)],
        out_specs=pl.BlockSpec((ts, D_MODEL), lambda i: (i, 0)),
        out_shape=jax.ShapeDtypeStruct((t, D_MODEL), F32),
        compiler_params=pltpu.CompilerParams(dimension_semantics=("parallel",),
                                             vmem_limit_bytes=VMEM_LIMIT),
        name="ffn",
    )(x2d, g1, w1, w2, g2)


def _split_w_in(w_in):
    n_if = 2 * MLSTM_HEADS
    c_if = SSM_WIDTH + 4 * MLSTM_WIDTH
    assert c_if == C_AQ and w_in.shape[-1] - n_if == C_GIF
    w = w_in.astype(BF16)
    wa = w[..., :c_if]
    wb = w[..., c_if + n_if:]
    wg = jnp.pad(w[..., c_if:c_if + n_if], ((0, 0), (0, 0), (0, LANES - n_if)))
    return wa, wb, wg


def kernel(x, w_in, conv_w, ssm_a_re, ssm_a_im, ssm_log_dt, ssm_b_re, ssm_b_im, ssm_c_re, ssm_c_im, ssm_d, ssm_w_glu, mlstm_i_bias, mlstm_f_bias, mlstm_head_gain, rel_bias, w_ssm_proj, w_mlstm_proj, w_moba_proj, w_out, w_ff1, w_ff2, norm_mix_pre, norm_mix_post, norm_ffn_pre, norm_ffn_post):
    bsz, seq, _ = x.shape
    depth = w_in.shape[0]
    assert bsz == SUBLANES and seq % MOBA_BLOCK == 0 and seq // MOBA_BLOCK <= SUBLANES
    ts_in = 256
    ts = 512
    tc = 64
    x2d = x.reshape(bsz * seq, D_MODEL)
    wa, wb, wg = _split_w_in(w_in)
    wps, wpm, wpa = w_ssm_proj.astype(BF16), w_mlstm_proj.astype(BF16), w_moba_proj.astype(BF16)
    wo = w_out.astype(BF16)
    w1 = w_ff1.astype(BF16)
    w2 = w_ff2.astype(BF16)
    wglu = ssm_w_glu.astype(BF16)
    gate_bias = jnp.pad(jnp.concatenate([mlstm_i_bias, mlstm_f_bias], axis=-1).astype(F32),
                        ((0, 0), (0, LANES - 2 * MLSTM_HEADS)))
    tiles, far = _moba_bias_tiles(rel_bias)
    for l in range(depth):
        u, qk, mv, mo, aq, ak, av, gate, gif = _inproj(x2d, norm_mix_pre[l][None, :], wa[l], wb[l], wg[l],
                                                       ts_in)
        bblk, cblk, ar_b, ai_b = _ssm_params(ssm_a_re[l], ssm_a_im[l], ssm_log_dt[l], ssm_b_re[l], ssm_b_im[l],
                                             ssm_c_re[l], ssm_c_im[l], bsz)
        y_ssm = _ssm(u.reshape(bsz, seq, SSM_WIDTH), bblk, cblk, ar_b, ai_b, ssm_d[l][None, :], wglu[l],
                     bsz, seq, tc)
        y_mlstm = _mlstm(qk, mv, mo, gif, conv_w[l].astype(F32), gate_bias[l][None, :],
                         mlstm_head_gain[l][None, :].astype(F32), bsz, seq)
        y_moba = _moba(aq, ak, av, tiles, far, bsz, seq)
        x2d = _merge(x2d, y_ssm.reshape(bsz * seq, SSM_WIDTH), y_mlstm, y_moba, gate, wps[l], wpm[l], wpa[l],
                     wo[l], norm_mix_post[l][None, :], ts)
        x2d = _ffn(x2d, norm_ffn_pre[l][None, :], w1[l], w2[l], norm_ffn_post[l][None, :], ts)
    return x2d.reshape(bsz, seq, D_MODEL)
```
